```python
import jax, jax.numpy as jnp
from jax import lax
import numpy as np


D_MODEL = 1024
BATCH = 8
SEQ = 4096
DEPTH = 1

CHUNK = 64
Q_BLOCK = 64
ROPE_THETA = 500000.0
EPS = 1e-6

A_HEADS = 8
A_HEAD_DIM = 64
A_WIDTH = A_HEADS * A_HEAD_DIM
IDX_HEADS = 16
IDX_DIM = 64
TOPK_MAX = 256

B_HEADS = 4
B_HEAD_DIM = 128
B_WIDTH = B_HEADS * B_HEAD_DIM
CONV_WIDTH = 4

IN_SIZES = (
    A_WIDTH, A_WIDTH, A_WIDTH, A_WIDTH,
    IDX_HEADS * IDX_DIM, IDX_DIM, IDX_HEADS,
    B_WIDTH, B_WIDTH, B_WIDTH, B_WIDTH, B_WIDTH,
    B_HEADS, B_HEADS,
    D_MODEL, D_MODEL,
)
N_IN = sum(IN_SIZES)

kernel_name = 'hybrid_dsa_mlstm_gated_merge'


def rmsnorm(x, g):
    x32 = x.astype(jnp.float32)
    y = x32 * lax.rsqrt(jnp.mean(x32 * x32, axis=-1, keepdims=True) + EPS)
    return (y * g.astype(jnp.float32)).astype(x.dtype)


def head_norm(x, g):
    shp = x.shape
    x32 = x.astype(jnp.float32).reshape(shp[:-1] + (B_HEADS, B_HEAD_DIM))
    mu = jnp.mean(x32, axis=-1, keepdims=True)
    var = jnp.mean(jnp.square(x32 - mu), axis=-1, keepdims=True)
    y = ((x32 - mu) * lax.rsqrt(var + EPS)).reshape(shp)
    return (y * g.astype(jnp.float32)).astype(x.dtype)


def partial_rope(x, pos):
    d = x.shape[-1]
    rd = d // 4
    half = rd // 2
    inv = ROPE_THETA ** (-(jnp.arange(half, dtype=jnp.float32) * 2.0 / rd))
    ang = pos.astype(jnp.float32)[:, None] * inv[None, :]
    cos = jnp.cos(ang)[:, None, :]
    sin = jnp.sin(ang)[:, None, :]
    x32 = x.astype(jnp.float32)
    x1, x2, rest = x32[..., :half], x32[..., half:rd], x32[..., rd:]
    out = jnp.concatenate([x1 * cos - x2 * sin, x1 * sin + x2 * cos, rest], axis=-1)
    return out.astype(x.dtype)


def causal_conv(u, w, b):
    K = w.shape[0]
    S = u.shape[1]
    up = jnp.pad(u, ((0, 0), (K - 1, 0), (0, 0)))
    return sum(up[:, j:j + S] * w[j] for j in range(K)) + b


def dsa_attention(q, k, v, qi, ki, wi, k_sel):
    Bsz, S, H, Dh = q.shape
    nblk = S // Q_BLOCK
    kchunk = jnp.arange(S) // CHUNK
    ki32 = ki.astype(jnp.float32)

    def blocks(a):
        return jnp.moveaxis(a.reshape((Bsz, nblk, Q_BLOCK) + a.shape[2:]), 1, 0)

    qpos = jnp.arange(S).reshape(nblk, Q_BLOCK)

    def one_block(args):
        q_b, qi_b, wi_b, pos_b = args
        s_idx = jnp.einsum('bthd,bsd->bths', qi_b.astype(jnp.float32), ki32) * (IDX_DIM ** -0.5)
        score = jnp.einsum('bth,bths->bts', wi_b.astype(jnp.float32), jax.nn.relu(s_idx))
        qchunk = pos_b // CHUNK
        admissible = kchunk[None, :] <= qchunk[:, None]
        score = jnp.where(admissible[None], score, -jnp.inf)
        _, idx = lax.top_k(score, k_sel)
        valid = (idx // CHUNK) <= qchunk[None, :, None]
        k_g = jax.vmap(lambda kb, ib: kb[ib])(k, idx)
        v_g = jax.vmap(lambda vb, ib: vb[ib])(v, idx)
        logits = jnp.einsum('bthd,btjhd->bthj', q_b.astype(jnp.float32),
                            k_g.astype(jnp.float32)) * (Dh ** -0.5)
        logits = jnp.where(valid[:, :, None, :], logits, -jnp.inf)
        p = jax.nn.softmax(logits, axis=-1)
        out = jnp.einsum('bthj,btjhd->bthd', p, v_g.astype(jnp.float32))
        return out.astype(q.dtype)

    out = lax.map(one_block, (blocks(q), blocks(qi), blocks(wi), qpos))
    return jnp.moveaxis(out, 0, 1).reshape(Bsz, S, H * Dh)


def mlstm_chunkwise(q, k, v, log_i, log_f):
    Bsz, H, S, dk = q.shape
    dv = v.shape[-1]
    nc = S // CHUNK

    def to_chunks(a):
        a = a.astype(jnp.float32)
        return jnp.moveaxis(a.reshape((Bsz, H, nc, CHUNK) + a.shape[3:]), 2, 0)

    xs = (to_chunks(q), to_chunks(k), to_chunks(v), to_chunks(log_i), to_chunks(log_f))
    causal = jnp.tril(jnp.ones((CHUNK, CHUNK), dtype=bool))

    def step(carry, xc):
        C, n, m = carry
        qc, kc, vc, li, lf = xc
        b = jnp.cumsum(lf, axis=-1)
        D = jnp.where(causal, b[..., :, None] - b[..., None, :] + li[..., None, :], -jnp.inf)
        m_inter = b + m[..., None]
        m_t = jnp.maximum(m_inter, jnp.max(D, axis=-1))
        w_inter = jnp.exp(m_inter - m_t)
        P = jnp.exp(D - m_t[..., None])
        s_qk = jnp.einsum('bhtd,bhsd->bhts', qc, kc) * P
        num = (w_inter[..., None] * jnp.einsum('bhtd,bhde->bhte', qc, C)
               + jnp.einsum('bhts,bhse->bhte', s_qk, vc))
        den = w_inter * jnp.einsum('bhtd,bhd->bht', qc, n) + jnp.sum(s_qk, axis=-1)
        h = num / jnp.maximum(jnp.abs(den), jnp.exp(-m_t))[..., None]
        b_L = b[..., -1]
        g = b_L[..., None] - b + li
        m_new = jnp.maximum(b_L + m, jnp.max(g, axis=-1))
        decay = jnp.exp(b_L + m - m_new)
        wk = jnp.exp(g - m_new[..., None])
        C_new = decay[..., None, None] * C + jnp.einsum('bhs,bhsd,bhse->bhde', wk, kc, vc)
        n_new = decay[..., None] * n + jnp.einsum('bhs,bhsd->bhd', wk, kc)
        return (C_new, n_new, m_new), h

    init = (jnp.zeros((Bsz, H, dk, dv), jnp.float32),
            jnp.zeros((Bsz, H, dk), jnp.float32),
            jnp.zeros((Bsz, H), jnp.float32))
    _, hs = lax.scan(step, init, xs)
    return jnp.moveaxis(hs, 0, 2).reshape(Bsz, H, S, dv)


def setup_inputs(seed: int = 0) -> dict:
    key = jax.random.key(seed)
    ks = jax.random.split(key, 12)
    f32 = jnp.float32
    x = jax.random.normal(ks[0], (BATCH, SEQ, D_MODEL), f32)
    norm_g = 1.0 + 0.05 * jax.random.normal(ks[1], (DEPTH, D_MODEL), f32)
    w_in = jax.random.normal(ks[2], (DEPTH, D_MODEL, N_IN), f32) * D_MODEL ** -0.5
    conv_w = jax.random.normal(ks[3], (DEPTH, CONV_WIDTH, 2 * B_WIDTH), f32) * CONV_WIDTH ** -0.5
    conv_b = 0.02 * jax.random.normal(ks[4], (DEPTH, 2 * B_WIDTH), f32)
    b_igate = 0.1 * jax.random.normal(ks[5], (DEPTH, B_HEADS), f32)
    b_fgate = 3.0 + 3.0 * jax.random.uniform(ks[6], (DEPTH, B_HEADS), f32)
    head_norm_g = 1.0 + 0.05 * jax.random.normal(ks[7], (DEPTH, B_WIDTH), f32)
    w_branch_a = jax.random.normal(ks[8], (DEPTH, A_WIDTH, D_MODEL), f32) * A_WIDTH ** -0.5
    w_branch_b = jax.random.normal(ks[9], (DEPTH, B_WIDTH, D_MODEL), f32) * B_WIDTH ** -0.5
    w_out = jax.random.normal(ks[10], (DEPTH, D_MODEL, D_MODEL), f32) * D_MODEL ** -0.5
    final_norm_g = 1.0 + 0.05 * jax.random.normal(ks[11], (D_MODEL,), f32)
    return {'x': x, 'norm_g': norm_g, 'w_in': w_in, 'conv_w': conv_w, 'conv_b': conv_b,
            'b_igate': b_igate, 'b_fgate': b_fgate, 'head_norm_g': head_norm_g,
            'w_branch_a': w_branch_a, 'w_branch_b': w_branch_b, 'w_out': w_out,
            'final_norm_g': final_norm_g}


def reference(x, norm_g, w_in, conv_w, conv_b, b_igate, b_fgate, head_norm_g,
              w_branch_a, w_branch_b, w_out, final_norm_g):
    Bsz, S, _ = x.shape
    pos = jnp.arange(S)
    k_sel = min(TOPK_MAX, S // 4)
    offsets = np.cumsum(IN_SIZES)[:-1]
    for l in range(DEPTH):
        h = rmsnorm(x, norm_g[l])
        (a_q, a_k, a_v, a_z, i_q, i_k, i_w,
         b_q, b_k, b_v, b_z, b_o, b_i, b_f, g_a, g_b) = jnp.split(h @ w_in[l], offsets, axis=-1)

        q_a = partial_rope(a_q.reshape(Bsz, S, A_HEADS, A_HEAD_DIM), pos)
        k_a = partial_rope(a_k.reshape(Bsz, S, A_HEADS, A_HEAD_DIM), pos)
        v_a = a_v.reshape(Bsz, S, A_HEADS, A_HEAD_DIM)
        q_i = partial_rope(i_q.reshape(Bsz, S, IDX_HEADS, IDX_DIM), pos)
        k_i = partial_rope(i_k[:, :, None, :], pos)[:, :, 0]
        w_i = i_w * IDX_HEADS ** -0.5
        o_a = dsa_attention(q_a, k_a, v_a, q_i, k_i, w_i, k_sel)
        y_a = (o_a * jax.nn.silu(a_z)) @ w_branch_a[l]

        qk_b = jax.nn.silu(causal_conv(jnp.concatenate([b_q, b_k], axis=-1), conv_w[l], conv_b[l]))
        q_b, k_b = jnp.split(qk_b, 2, axis=-1)

        def heads(t):
            return t.reshape(Bsz, S, B_HEADS, B_HEAD_DIM).transpose(0, 2, 1, 3)

        log_i = (b_i + b_igate[l]).astype(jnp.float32).transpose(0, 2, 1)
        log_f = jax.nn.log_sigmoid((b_f + b_fgate[l]).astype(jnp.float32)).transpose(0, 2, 1)
        cell = mlstm_chunkwise(heads(q_b), heads(k_b) * B_HEAD_DIM ** -0.5, heads(b_v), log_i, log_f)
        cell = cell.transpose(0, 2, 1, 3).reshape(Bsz, S, B_WIDTH).astype(x.dtype)
        h_b = head_norm(jax.nn.sigmoid(b_o) * cell, head_norm_g[l])
        y_b = (h_b * jax.nn.silu(b_z)) @ w_branch_b[l]

        merged = jax.nn.sigmoid(g_a) * y_a + jax.nn.sigmoid(g_b) * y_b
        x = x + merged @ w_out[l]
    return rmsnorm(x, final_norm_g)
```

```python
import functools

import jax
import jax.numpy as jnp
import numpy as np
from jax import lax
from jax.experimental import pallas as pl
from jax.experimental.pallas import tpu as pltpu

D_MODEL = 1024
CHUNK = 64
ROPE_THETA = 500000.0
EPS = 1e-6
A_HEADS, A_HEAD_DIM = 8, 64
A_WIDTH = A_HEADS * A_HEAD_DIM
IDX_HEADS, IDX_DIM = 16, 64
IDX_WIDTH = IDX_HEADS * IDX_DIM
TOPK_MAX = 256
B_HEADS, B_HEAD_DIM = 4, 128
B_WIDTH = B_HEADS * B_HEAD_DIM
CONV_WIDTH = 4
IN_SIZES = (A_WIDTH,) * 4 + (IDX_WIDTH, IDX_DIM, IDX_HEADS) + (B_WIDTH,) * 5 \
    + (B_HEADS, B_HEADS, D_MODEL, D_MODEL)

LANES = 128
SUBLANES = 8
VMEM_LIMIT_BYTES = 56 * 1024 * 1024

PROJ_ROWS = 512
CONV_HALO = SUBLANES
DSA_TQ = 256
DSA_TS = 512
MLSTM_L = 128
MLSTM_ROWS = 1024
OUT_ROWS = 256
ROPE_HALF = A_HEAD_DIM // 8
SMALL_ROWS = 32
INT_MIN = -2 ** 31
NEG_BIG = -1e30

assert CONV_HALO >= CONV_WIDTH - 1 and IDX_DIM == A_HEAD_DIM == LANES // 2 and B_HEAD_DIM == LANES

_f32 = jnp.float32
_bf16 = jnp.bfloat16
_i32 = jnp.int32


def _dot(a, b):
    return jnp.dot(a, b, preferred_element_type=_f32)


def _dot_nt(a, b):
    return lax.dot_general(a, b, (((1,), (1,)), ((), ())), preferred_element_type=_f32)


def _dot_tn(a, b):
    return lax.dot_general(a, b, (((0,), (0,)), ((), ())), preferred_element_type=_f32)


def _split3(x):
    hi = x.astype(_bf16)
    r1 = x - hi.astype(_f32)
    mid = r1.astype(_bf16)
    lo = (r1 - mid.astype(_f32)).astype(_bf16)
    return hi, mid, lo


def _rmsnorm(x, g):
    ms = jnp.mean(x * x, axis=-1, keepdims=True)
    return x * lax.rsqrt(ms + EPS) * g


def _silu(x):
    return x * jax.nn.sigmoid(x)


def _params(n_axes):
    return pltpu.CompilerParams(dimension_semantics=("arbitrary",) * n_axes,
                                vmem_limit_bytes=VMEM_LIMIT_BYTES)


_C_AQ = 0
_C_AK = _C_AQ + A_WIDTH
_C_IQ = _C_AK + A_WIDTH
_C_BQK = _C_IQ + IDX_WIDTH
_C_BV = _C_BQK + 2 * B_WIDTH
_C_IK2 = _C_BV + B_WIDTH
_C_SMALL = _C_IK2 + LANES
_C_END = _C_SMALL + LANES
_S_LI, _S_LF, _S_W = 0, B_HEADS, 2 * B_HEADS


def _rope(x, cos_t, sa_t, sb_t):
    outs = []
    for g in range(x.shape[1] // LANES):
        xs = x[:, g * LANES:(g + 1) * LANES]
        up = pltpu.roll(xs, LANES - ROPE_HALF, 1)
        dn = pltpu.roll(xs, ROPE_HALF, 1)
        outs.append(xs * cos_t + up * sa_t + dn * sb_t)
    return outs[0] if len(outs) == 1 else jnp.concatenate(outs, axis=1)


def _finish_small(t, idx):
    lf = jax.nn.log_sigmoid(t)
    return jnp.where((idx >= _S_LF) & (idx < _S_W), lf,
                     jnp.where(idx >= _S_W, t * (IDX_HEADS ** -0.5), t))


def _proj_kernel(x_ref, g_ref, w_ref, wvt_ref, wst_ref, cos_ref, sa_ref, sb_ref, cw_ref, cb_ref,
                 bias_ref, biast_ref,
                 qa_ref, ka_ref, vat_ref, qi_ref, ki2_ref, qb_ref, kb_ref, vb_ref,
                 small_ref, smallt_ref, ubuf, *, tiles_per_seq):
    i = pl.program_id(0)
    rows = x_ref.shape[0]
    h = _rmsnorm(x_ref[...], g_ref[...]).astype(_bf16)
    cos_t, sa_t, sb_t = cos_ref[...], sa_ref[...], sb_ref[...]

    qa = _rope(_dot(h, w_ref[:, _C_AQ:_C_AQ + A_WIDTH]), cos_t, sa_t, sb_t)
    qa_ref[...] = (qa * (A_HEAD_DIM ** -0.5)).astype(_bf16)
    ka_ref[...] = _rope(_dot(h, w_ref[:, _C_AK:_C_AK + A_WIDTH]), cos_t, sa_t, sb_t).astype(_bf16)
    vat_ref[...] = _dot_nt(wvt_ref[...], h).astype(_bf16)
    qi = _rope(_dot(h, w_ref[:, _C_IQ:_C_IQ + IDX_WIDTH]), cos_t, sa_t, sb_t)
    qi_ref[...] = (qi * (IDX_DIM ** -0.5)).astype(_bf16)
    ki2_ref[...] = _rope(_dot(h, w_ref[:, _C_IK2:_C_IK2 + LANES]), cos_t, sa_t, sb_t).astype(_bf16)

    @pl.when(i % tiles_per_seq == 0)
    def _():
        ubuf[0:CONV_HALO, :] = jnp.zeros((CONV_HALO, 2 * B_WIDTH), _f32)

    ubuf[CONV_HALO:CONV_HALO + rows, :] = _dot(h, w_ref[:, _C_BQK:_C_BQK + 2 * B_WIDTH])
    conv = None
    for j in range(CONV_WIDTH):
        off = CONV_HALO - (CONV_WIDTH - 1) + j
        term = ubuf[off:off + rows, :] * cw_ref[j:j + 1, :]
        conv = term if conv is None else conv + term
    conv = conv + cb_ref[...]
    ubuf[0:CONV_HALO, :] = ubuf[rows:rows + CONV_HALO, :]
    qk = _silu(conv)
    qb_ref[...] = qk[:, :B_WIDTH].astype(_bf16)
    kb_ref[...] = (qk[:, B_WIDTH:] * (B_HEAD_DIM ** -0.5)).astype(_bf16)
    vb_ref[...] = _dot(h, w_ref[:, _C_BV:_C_BV + B_WIDTH]).astype(_bf16)

    sm = _dot(h, w_ref[:, _C_SMALL:_C_SMALL + LANES]) + bias_ref[...]
    small_ref[...] = _finish_small(sm, lax.broadcasted_iota(_i32, sm.shape, 1))
    smt = _dot_nt(wst_ref[...], h) + biast_ref[:, 0:1]
    smallt_ref[...] = _finish_small(smt, lax.broadcasted_iota(_i32, smt.shape, 0))


def _proj_call(x2, g, w, wvt, wst, cos_t, sa_t, sb_t, cw, cb, bias, biast, *, seq):
    tokens = x2.shape[0]
    rows = min(PROJ_ROWS, seq)
    tiles_per_seq = seq // rows
    grid = (tokens // rows,)
    const = lambda i: (0, 0)
    tok = lambda i: (i, 0)
    tok_t = lambda i: (0, i)
    pos = lambda i: (i % tiles_per_seq, 0)
    in_specs = [
        pl.BlockSpec((rows, D_MODEL), tok),
        pl.BlockSpec((1, D_MODEL), const),
        pl.BlockSpec((D_MODEL, _C_END), const),
        pl.BlockSpec((A_WIDTH, D_MODEL), const),
        pl.BlockSpec((SMALL_ROWS, D_MODEL), const),
        pl.BlockSpec((rows, LANES), pos),
        pl.BlockSpec((rows, LANES), pos),
        pl.BlockSpec((rows, LANES), pos),
        pl.BlockSpec((CONV_WIDTH, 2 * B_WIDTH), const),
        pl.BlockSpec((1, 2 * B_WIDTH), const),
        pl.BlockSpec((1, LANES), const),
        pl.BlockSpec((SMALL_ROWS, LANES), const),
    ]
    out_shape = [
        jax.ShapeDtypeStruct((tokens, A_WIDTH), _bf16),
        jax.ShapeDtypeStruct((tokens, A_WIDTH), _bf16),
        jax.ShapeDtypeStruct((A_WIDTH, tokens), _bf16),
        jax.ShapeDtypeStruct((tokens, IDX_WIDTH), _bf16),
        jax.ShapeDtypeStruct((tokens, LANES), _bf16),
        jax.ShapeDtypeStruct((tokens, B_WIDTH), _bf16),
        jax.ShapeDtypeStruct((tokens, B_WIDTH), _bf16),
        jax.ShapeDtypeStruct((tokens, B_WIDTH), _bf16),
        jax.ShapeDtypeStruct((tokens, LANES), _f32),
        jax.ShapeDtypeStruct((SMALL_ROWS, tokens), _f32),
    ]
    out_specs = [
        pl.BlockSpec((rows, A_WIDTH), tok),
        pl.BlockSpec((rows, A_WIDTH), tok),
        pl.BlockSpec((A_WIDTH, rows), tok_t),
        pl.BlockSpec((rows, IDX_WIDTH), tok),
        pl.BlockSpec((rows, LANES), tok),
        pl.BlockSpec((rows, B_WIDTH), tok),
        pl.BlockSpec((rows, B_WIDTH), tok),
        pl.BlockSpec((rows, B_WIDTH), tok),
        pl.BlockSpec((rows, LANES), tok),
        pl.BlockSpec((SMALL_ROWS, rows), tok_t),
    ]
    return pl.pallas_call(
        functools.partial(_proj_kernel, tiles_per_seq=tiles_per_seq),
        grid=grid, in_specs=in_specs, out_specs=out_specs, out_shape=out_shape,
        scratch_shapes=[pltpu.VMEM((rows + 2 * CONV_HALO, 2 * B_WIDTH), _f32)],
        compiler_params=_params(1), name="proj",
    )(x2, g, w, wvt, wst, cos_t, sa_t, sb_t, cw, cb, bias, biast)


def _sortable(x):
    bits = pltpu.bitcast(x, _i32)
    return jnp.where(bits >= 0, bits, bits ^ jnp.int32(0x7FFFFFFF))


def _dsa_kernel(qi_ref, qa_ref, wt_ref, ki2_ref, ka_ref, vat_ref, o_ref,
                skey, qim, qam, m_sc, l_sc, acc_sc, *, k_sel, ts):
    qt = pl.program_id(1)
    tq = qi_ref.shape[0]
    q0 = qt * tq
    n_kt = (q0 + tq + ts - 1) // ts

    lo_half = lax.broadcasted_iota(_i32, (tq, LANES), 1) < (LANES // 2)
    zero = jnp.zeros((tq, LANES), _bf16)
    for p in range(IDX_HEADS // 2):
        pair = qi_ref[:, p * LANES:(p + 1) * LANES]
        qim[2 * p] = jnp.where(lo_half, pair, zero)
        qim[2 * p + 1] = jnp.where(lo_half, zero, pair)
    for p in range(A_HEADS // 2):
        pair = qa_ref[:, p * LANES:(p + 1) * LANES]
        qam[2 * p] = jnp.where(lo_half, pair, zero)
        qam[2 * p + 1] = jnp.where(lo_half, zero, pair)

    qchunk = (q0 + lax.broadcasted_iota(_i32, (ts, tq), 1)) // CHUNK
    krow = lax.broadcasted_iota(_i32, (ts, tq), 0)

    def score_body(kt, carry):
        k0 = pl.multiple_of(kt * ts, ts)
        kslab = ki2_ref[pl.ds(k0, ts), :]
        acc = jnp.zeros((ts, tq), _f32)
        for hh in range(IDX_HEADS):
            s = _dot_nt(kslab, qim[hh])
            acc = acc + jnp.maximum(s, 0.0) * wt_ref[_S_W + hh:_S_W + hh + 1, :]
        admissible = (k0 + krow) // CHUNK <= qchunk
        skey[pl.ds(k0, ts), :] = jnp.where(admissible, _sortable(acc), jnp.int32(INT_MIN))
        return carry

    lax.fori_loop(0, n_kt, score_body, 0)

    def bit_body(i, thr):
        cand = thr + jnp.left_shift(jnp.int32(1), 31 - i)

        def count_body(kt, cnt):
            k0 = pl.multiple_of(kt * ts, ts)
            ge = jnp.where(skey[pl.ds(k0, ts), :] >= cand, jnp.int32(1), jnp.int32(0))
            return cnt + ge.reshape(ts // SUBLANES, SUBLANES, tq).sum(axis=0)

        cnt = lax.fori_loop(0, n_kt, count_body, jnp.zeros((SUBLANES, tq), _i32))
        total = cnt.sum(axis=0, keepdims=True)
        return jnp.where(total >= k_sel, cand, thr)

    thr = lax.fori_loop(0, 32, bit_body, jnp.full((1, tq), INT_MIN, _i32))
    thr = jnp.maximum(thr, jnp.int32(INT_MIN + 1))

    m_sc[...] = jnp.full(m_sc.shape, NEG_BIG, _f32)
    l_sc[...] = jnp.zeros(l_sc.shape, _f32)
    acc_sc[...] = jnp.zeros(acc_sc.shape, _f32)

    def att_body(kt, carry):
        k0 = pl.multiple_of(kt * ts, ts)
        sel = skey[pl.ds(k0, ts), :] >= thr
        for hh in range(A_HEADS):
            p = hh // 2
            kpair = ka_ref[pl.ds(k0, ts), p * LANES:(p + 1) * LANES]
            lg = jnp.where(sel, _dot_nt(kpair, qam[hh]), -jnp.inf)
            m_old = m_sc[hh:hh + 1, :]
            m_new = jnp.maximum(m_old, lg.max(axis=0, keepdims=True))
            alpha = jnp.exp(m_old - m_new)
            pr = jnp.exp(lg - m_new)
            l_sc[hh:hh + 1, :] = alpha * l_sc[hh:hh + 1, :] + pr.sum(axis=0, keepdims=True)
            rows = slice(hh * A_HEAD_DIM, (hh + 1) * A_HEAD_DIM)
            pv = _dot(vat_ref[rows, pl.ds(k0, ts)], pr.astype(_bf16))
            acc_sc[rows, :] = alpha * acc_sc[rows, :] + pv
            m_sc[hh:hh + 1, :] = m_new
        return carry

    lax.fori_loop(0, n_kt, att_body, 0)

    outs = []
    for hh in range(A_HEADS):
        rows = slice(hh * A_HEAD_DIM, (hh + 1) * A_HEAD_DIM)
        outs.append(acc_sc[rows, :] / l_sc[hh:hh + 1, :])
    o_ref[...] = jnp.concatenate(outs, axis=0).T


def _dsa_call(qi, qa, smallt, ki2, ka, vat, *, batch, seq, k_sel):
    tq = min(DSA_TQ, seq)
    ts = min(DSA_TS, seq)
    nq = seq // tq
    qtok = lambda b, q: (b * nq + q, 0)
    return pl.pallas_call(
        functools.partial(_dsa_kernel, k_sel=k_sel, ts=ts),
        grid=(batch, nq),
        in_specs=[
            pl.BlockSpec((tq, IDX_WIDTH), qtok),
            pl.BlockSpec((tq, A_WIDTH), qtok),
            pl.BlockSpec((SMALL_ROWS, tq), lambda b, q: (0, b * nq + q)),
            pl.BlockSpec((seq, LANES), lambda b, q: (b, 0)),
            pl.BlockSpec((seq, A_WIDTH), lambda b, q: (b, 0)),
            pl.BlockSpec((A_WIDTH, seq), lambda b, q: (0, b)),
        ],
        out_specs=pl.BlockSpec((tq, A_WIDTH), qtok),
        out_shape=jax.ShapeDtypeStruct((batch * seq, A_WIDTH), _f32),
        scratch_shapes=[
            pltpu.VMEM((seq, tq), _i32),
            pltpu.VMEM((IDX_HEADS, tq, LANES), _bf16),
            pltpu.VMEM((A_HEADS, tq, LANES), _bf16),
            pltpu.VMEM((A_HEADS, tq), _f32),
            pltpu.VMEM((A_HEADS, tq), _f32),
            pltpu.VMEM((A_WIDTH, tq), _f32),
        ],
        compiler_params=_params(2), name="dsa",
    )(qi, qa, smallt, ki2, ka, vat)


def _mlstm_kernel(q_ref, k_ref, v_ref, small_ref, smallt_ref, o_ref, c_sc, m_sc):
    rows = q_ref.shape[0]
    L = min(MLSTM_L, rows)
    d = B_HEAD_DIM

    @pl.when(pl.program_id(1) == 0)
    def _():
        c_sc[...] = jnp.zeros(c_sc.shape, _f32)
        m_sc[...] = jnp.zeros(m_sc.shape, _f32)

    tril = lax.broadcasted_iota(_i32, (L, L), 1) <= lax.broadcasted_iota(_i32, (L, L), 0)
    trilb = jnp.where(tril, 1.0, 0.0).astype(_bf16)
    lane0 = lax.broadcasted_iota(_i32, (L, d), 1) == 0

    def chunk_body(c, carry):
        r0 = pl.multiple_of(c * L, L)
        sm = small_ref[pl.ds(r0, L), :]
        smt = smallt_ref[:, pl.ds(r0, L)]
        bcol = sum(_dot(trilb, part) for part in _split3(sm))
        brow = sum(_dot_nt(part, trilb) for part in _split3(smt))
        for hh in range(B_HEADS):
            cols = slice(hh * d, (hh + 1) * d)
            li_c, b_c = sm[:, _S_LI + hh:_S_LI + hh + 1], bcol[:, _S_LF + hh:_S_LF + hh + 1]
            li_r, b_r = smt[_S_LI + hh:_S_LI + hh + 1, :], brow[_S_LF + hh:_S_LF + hh + 1, :]
            m_prev = m_sc[hh:hh + 1, 0:1]
            dmat = jnp.where(tril, b_c - b_r + li_r, -jnp.inf)
            m_inter = b_c + m_prev
            m_t = jnp.maximum(m_inter, dmat.max(axis=1, keepdims=True))
            w_inter = jnp.exp(m_inter - m_t)
            qh = q_ref[pl.ds(r0, L), cols]
            kh = k_ref[pl.ds(r0, L), cols]
            vh = v_ref[pl.ds(r0, L), cols]
            s_qk = _dot_nt(qh, kh) * jnp.exp(dmat - m_t)
            cext = c_sc[hh]
            inter = _dot(qh, cext.astype(_bf16))
            num = w_inter * inter[:, :d] + _dot(s_qk.astype(_bf16), vh)
            den = w_inter * inter[:, d:d + 1] + s_qk.sum(axis=1, keepdims=True)
            o_ref[pl.ds(r0, L), cols] = num / jnp.maximum(jnp.abs(den), jnp.exp(-m_t))
            b_last = b_c[L - 1:L, :]
            g_c = b_last - b_c + li_c
            m_new = jnp.maximum(b_last + m_prev, g_c.max(axis=0, keepdims=True))
            decay = jnp.exp(b_last + m_prev - m_new)
            wk = jnp.exp(g_c - m_new)
            wkv = jnp.concatenate([wk * vh.astype(_f32), jnp.where(lane0, wk, 0.0)], axis=1)
            c_sc[hh] = decay * cext + _dot_tn(kh, wkv.astype(_bf16))
            m_sc[hh:hh + 1, :] = jnp.broadcast_to(m_new, (1, LANES))
        return carry

    lax.fori_loop(0, rows // L, chunk_body, 0)


def _mlstm_call(qb, kb, vb, small, smallt, *, batch, seq):
    rows = min(MLSTM_ROWS, seq)
    nb = seq // rows
    tok = lambda b, j: (b * nb + j, 0)
    return pl.pallas_call(
        _mlstm_kernel,
        grid=(batch, nb),
        in_specs=[
            pl.BlockSpec((rows, B_WIDTH), tok),
            pl.BlockSpec((rows, B_WIDTH), tok),
            pl.BlockSpec((rows, B_WIDTH), tok),
            pl.BlockSpec((rows, LANES), tok),
            pl.BlockSpec((SMALL_ROWS, rows), lambda b, j: (0, b * nb + j)),
        ],
        out_specs=pl.BlockSpec((rows, B_WIDTH), tok),
        out_shape=jax.ShapeDtypeStruct((batch * seq, B_WIDTH), _f32),
        scratch_shapes=[
            pltpu.VMEM((B_HEADS, B_HEAD_DIM, 2 * B_HEAD_DIM), _f32),
            pltpu.VMEM((SUBLANES, LANES), _f32),
        ],
        compiler_params=_params(2), name="mlstm",
    )(qb, kb, vb, small, smallt)


_G_AZ, _G_BZ, _G_BO = 0, A_WIDTH, A_WIDTH + B_WIDTH
_G_GA = _G_BO + B_WIDTH
_G_GB = _G_GA + D_MODEL
_G_END = _G_GB + D_MODEL


def _out_kernel(x_ref, oa_ref, cell_ref, g_ref, wg_ref, wba_ref, wbb_ref, wo_ref, hng_ref, fg_ref,
                out_ref, *, final_norm):
    x = x_ref[...]
    h = _rmsnorm(x, g_ref[...]).astype(_bf16)
    a_z = _dot(h, wg_ref[:, _G_AZ:_G_AZ + A_WIDTH])
    y_a = _dot((oa_ref[...] * _silu(a_z)).astype(_bf16), wba_ref[...])

    b_o = _dot(h, wg_ref[:, _G_BO:_G_BO + B_WIDTH])
    t = jax.nn.sigmoid(b_o) * cell_ref[...]
    normed = []
    for hh in range(B_HEADS):
        th = t[:, hh * B_HEAD_DIM:(hh + 1) * B_HEAD_DIM]
        mu = jnp.mean(th, axis=-1, keepdims=True)
        var = jnp.mean(jnp.square(th - mu), axis=-1, keepdims=True)
        normed.append((th - mu) * lax.rsqrt(var + EPS))
    b_z = _dot(h, wg_ref[:, _G_BZ:_G_BZ + B_WIDTH])
    h_b = jnp.concatenate(normed, axis=1) * hng_ref[...] * _silu(b_z)
    y_b = _dot(h_b.astype(_bf16), wbb_ref[...])

    g_a = _dot(h, wg_ref[:, _G_GA:_G_GA + D_MODEL])
    g_b = _dot(h, wg_ref[:, _G_GB:_G_GB + D_MODEL])
    merged = jax.nn.sigmoid(g_a) * y_a + jax.nn.sigmoid(g_b) * y_b
    y = x + _dot(merged.astype(_bf16), wo_ref[...])
    if final_norm:
        y = _rmsnorm(y, fg_ref[...])
    out_ref[...] = y


def _out_call(x2, oa, cell, g, wg, wba, wbb, wo, hng, fg, *, final_norm):
    tokens = x2.shape[0]
    rows = min(OUT_ROWS, tokens)
    const = lambda i: (0, 0)
    tok = lambda i: (i, 0)
    return pl.pallas_call(
        functools.partial(_out_kernel, final_norm=final_norm),
        grid=(tokens // rows,),
        in_specs=[
            pl.BlockSpec((rows, D_MODEL), tok),
            pl.BlockSpec((rows, A_WIDTH), tok),
            pl.BlockSpec((rows, B_WIDTH), tok),
            pl.BlockSpec((1, D_MODEL), const),
            pl.BlockSpec((D_MODEL, _G_END), const),
            pl.BlockSpec((A_WIDTH, D_MODEL), const),
            pl.BlockSpec((B_WIDTH, D_MODEL), const),
            pl.BlockSpec((D_MODEL, D_MODEL), const),
            pl.BlockSpec((1, B_WIDTH), const),
            pl.BlockSpec((1, D_MODEL), const),
        ],
        out_specs=pl.BlockSpec((rows, D_MODEL), tok),
        out_shape=jax.ShapeDtypeStruct((tokens, D_MODEL), _f32),
        compiler_params=_params(1), name="out",
    )(x2, oa, cell, g, wg, wba, wbb, wo, hng, fg)


def _rope_tables(seq):
    inv = ROPE_THETA ** (-(jnp.arange(ROPE_HALF, dtype=_f32) * 2.0 / (2 * ROPE_HALF)))
    ang = jnp.arange(seq).astype(_f32)[:, None] * inv[None, :]
    cos, sin = jnp.cos(ang), jnp.sin(ang)
    pad = jnp.zeros((seq, A_HEAD_DIM - 2 * ROPE_HALF), _f32)
    zero = jnp.zeros((seq, ROPE_HALF), _f32)
    cos_h = jnp.concatenate([cos, cos, pad + 1.0], axis=1)
    sa_h = jnp.concatenate([-sin, zero, pad], axis=1)
    sb_h = jnp.concatenate([zero, sin, pad], axis=1)
    rep = LANES // A_HEAD_DIM
    return tuple(jnp.tile(t, (1, rep)) for t in (cos_h, sa_h, sb_h))


def kernel(x, norm_g, w_in, conv_w, conv_b, b_igate, b_fgate, head_norm_g,
           w_branch_a, w_branch_b, w_out, final_norm_g):
    batch, seq, _ = x.shape
    depth = w_in.shape[0]
    k_sel = min(TOPK_MAX, seq // 4)
    offsets = np.cumsum(IN_SIZES)[:-1]
    cos_t, sa_t, sb_t = _rope_tables(seq)
    x2 = x.reshape(batch * seq, D_MODEL)
    for l in range(depth):
        (a_q, a_k, a_v, a_z, i_q, i_k, i_w, b_q, b_k, b_v, b_z, b_o, b_i, b_f, g_a, g_b) = \
            jnp.split(w_in[l], offsets, axis=-1)
        small_w = jnp.concatenate(
            [b_i, b_f, i_w, jnp.zeros((D_MODEL, LANES - 2 * B_HEADS - IDX_HEADS), _f32)], axis=1)
        w_proj = jnp.concatenate([a_q, a_k, i_q, b_q, b_k, b_v, i_k, i_k, small_w], axis=1).astype(_bf16)
        wvt = a_v.T.astype(_bf16)
        wst = small_w[:, :SMALL_ROWS].T.astype(_bf16)
        bias = jnp.concatenate(
            [b_igate[l], b_fgate[l], jnp.zeros((LANES - 2 * B_HEADS,), _f32)])
        biast = jnp.broadcast_to(bias[:SMALL_ROWS, None], (SMALL_ROWS, LANES))
        w_gate = jnp.concatenate([a_z, b_z, b_o, g_a, g_b], axis=1).astype(_bf16)
        g = norm_g[l].reshape(1, D_MODEL)

        qa, ka, vat, qi, ki2, qb, kb, vb, small, smallt = _proj_call(
            x2, g, w_proj, wvt, wst, cos_t, sa_t, sb_t, conv_w[l], conv_b[l].reshape(1, -1),
            bias.reshape(1, LANES), biast, seq=seq)
        o_a = _dsa_call(qi, qa, smallt, ki2, ka, vat, batch=batch, seq=seq, k_sel=k_sel)
        cell = _mlstm_call(qb, kb, vb, small, smallt, batch=batch, seq=seq)
        x2 = _out_call(x2, o_a, cell, g, w_gate, w_branch_a[l].astype(_bf16),
                       w_branch_b[l].astype(_bf16), w_out[l].astype(_bf16),
                       head_norm_g[l].reshape(1, B_WIDTH), final_norm_g.reshape(1, D_MODEL),
                       final_norm=(l == depth - 1))
    return x2.reshape(batch, seq, D_MODEL)
```

```python
import functools

import jax
import jax.numpy as jnp
import numpy as np
from jax import lax
from jax.experimental import pallas as pl
from jax.experimental.pallas import tpu as pltpu

D_MODEL = 1024
CHUNK = 64
ROPE_THETA = 500000.0
EPS = 1e-6
A_HEADS, A_HEAD_DIM = 8, 64
A_WIDTH = A_HEADS * A_HEAD_DIM
IDX_HEADS, IDX_DIM = 16, 64
IDX_WIDTH = IDX_HEADS * IDX_DIM
TOPK_MAX = 256
B_HEADS, B_HEAD_DIM = 4, 128
B_WIDTH = B_HEADS * B_HEAD_DIM
CONV_WIDTH = 4
IN_SIZES = (A_WIDTH,) * 4 + (IDX_WIDTH, IDX_DIM, IDX_HEADS) + (B_WIDTH,) * 5 \
    + (B_HEADS, B_HEADS, D_MODEL, D_MODEL)

LANES = 128
SUBLANES = 8
VMEM_LIMIT_BYTES = 56 * 1024 * 1024

PROJ_ROWS = 512
CONV_HALO = SUBLANES
DSA_TQ = 256
DSA_TS = 512
MLSTM_L = 128
MLSTM_ROWS = 1024
OUT_ROWS = 256
ROPE_HALF = A_HEAD_DIM // 8
SMALL_ROWS = 32
INT_MIN = -2 ** 31
HALF_BITS = 16
HALF_MIN = -2 ** (HALF_BITS - 1)
HALF_MASK = 2 ** HALF_BITS - 1
PACKED_SUBLANES = 2 * SUBLANES
LOG2E = 1.4426950408889634
NEG_BIG = -1e30

assert CONV_HALO >= CONV_WIDTH - 1 and IDX_DIM == A_HEAD_DIM == LANES // 2 and B_HEAD_DIM == LANES

_f32 = jnp.float32
_bf16 = jnp.bfloat16
_i32 = jnp.int32
_i16 = jnp.int16


def _dot(a, b):
    return jnp.dot(a, b, preferred_element_type=_f32)


def _dot_nt(a, b):
    return lax.dot_general(a, b, (((1,), (1,)), ((), ())), preferred_element_type=_f32)


def _dot_tn(a, b):
    return lax.dot_general(a, b, (((0,), (0,)), ((), ())), preferred_element_type=_f32)


def _split3(x):
    hi = x.astype(_bf16)
    r1 = x - hi.astype(_f32)
    mid = r1.astype(_bf16)
    lo = (r1 - mid.astype(_f32)).astype(_bf16)
    return hi, mid, lo


def _rmsnorm(x, g):
    ms = jnp.mean(x * x, axis=-1, keepdims=True)
    return x * lax.rsqrt(ms + EPS) * g


def _silu(x):
    return x * jax.nn.sigmoid(x)


def _params(n_axes):
    return pltpu.CompilerParams(dimension_semantics=("arbitrary",) * n_axes,
                                vmem_limit_bytes=VMEM_LIMIT_BYTES)


_C_AQ = 0
_C_AK = _C_AQ + A_WIDTH
_C_IQ = _C_AK + A_WIDTH
_C_BQK = _C_IQ + IDX_WIDTH
_C_BV = _C_BQK + 2 * B_WIDTH
_C_IK2 = _C_BV + B_WIDTH
_C_SMALL = _C_IK2 + LANES
_C_END = _C_SMALL + LANES
_S_LI, _S_LF, _S_W = 0, B_HEADS, 2 * B_HEADS


def _rope(x, cos_t, sa_t, sb_t):
    outs = []
    for g in range(x.shape[1] // LANES):
        xs = x[:, g * LANES:(g + 1) * LANES]
        up = pltpu.roll(xs, LANES - ROPE_HALF, 1)
        dn = pltpu.roll(xs, ROPE_HALF, 1)
        outs.append(xs * cos_t + up * sa_t + dn * sb_t)
    return outs[0] if len(outs) == 1 else jnp.concatenate(outs, axis=1)


def _finish_small(t, idx):
    lf = jax.nn.log_sigmoid(t)
    return jnp.where((idx >= _S_LF) & (idx < _S_W), lf,
                     jnp.where(idx >= _S_W, t * (IDX_HEADS ** -0.5), t))


def _proj_kernel(x_ref, g_ref, w_ref, wvt_ref, wst_ref, cos_ref, sa_ref, sb_ref, cw_ref, cb_ref,
                 bias_ref, biast_ref,
                 qa_ref, ka_ref, vat_ref, qi_ref, ki2_ref, qb_ref, kb_ref, vb_ref,
                 small_ref, smallt_ref, ubuf, *, tiles_per_seq):
    i = pl.program_id(0)
    rows = x_ref.shape[0]
    h = _rmsnorm(x_ref[...], g_ref[...]).astype(_bf16)
    cos_t, sa_t, sb_t = cos_ref[...], sa_ref[...], sb_ref[...]

    qa = _rope(_dot(h, w_ref[:, _C_AQ:_C_AQ + A_WIDTH]), cos_t, sa_t, sb_t)
    qa_ref[...] = (qa * (A_HEAD_DIM ** -0.5 * LOG2E)).astype(_bf16)
    ka_ref[...] = _rope(_dot(h, w_ref[:, _C_AK:_C_AK + A_WIDTH]), cos_t, sa_t, sb_t).astype(_bf16)
    vat_ref[...] = _dot_nt(wvt_ref[...], h).astype(_bf16)
    qi = _rope(_dot(h, w_ref[:, _C_IQ:_C_IQ + IDX_WIDTH]), cos_t, sa_t, sb_t)
    qi_ref[...] = (qi * (IDX_DIM ** -0.5)).astype(_bf16)
    ki2_ref[...] = _rope(_dot(h, w_ref[:, _C_IK2:_C_IK2 + LANES]), cos_t, sa_t, sb_t).astype(_bf16)

    @pl.when(i % tiles_per_seq == 0)
    def _():
        ubuf[0:CONV_HALO, :] = jnp.zeros((CONV_HALO, 2 * B_WIDTH), _f32)

    ubuf[CONV_HALO:CONV_HALO + rows, :] = _dot(h, w_ref[:, _C_BQK:_C_BQK + 2 * B_WIDTH])
    conv = None
    for j in range(CONV_WIDTH):
        off = CONV_HALO - (CONV_WIDTH - 1) + j
        term = ubuf[off:off + rows, :] * cw_ref[j:j + 1, :]
        conv = term if conv is None else conv + term
    conv = conv + cb_ref[...]
    ubuf[0:CONV_HALO, :] = ubuf[rows:rows + CONV_HALO, :]
    qk = _silu(conv)
    qb_ref[...] = qk[:, :B_WIDTH].astype(_bf16)
    kb_ref[...] = (qk[:, B_WIDTH:] * (B_HEAD_DIM ** -0.5)).astype(_bf16)
    vb_ref[...] = _dot(h, w_ref[:, _C_BV:_C_BV + B_WIDTH]).astype(_bf16)

    sm = _dot(h, w_ref[:, _C_SMALL:_C_SMALL + LANES]) + bias_ref[...]
    small_ref[...] = _finish_small(sm, lax.broadcasted_iota(_i32, sm.shape, 1))
    smt = _dot_nt(wst_ref[...], h) + biast_ref[:, 0:1]
    smallt_ref[...] = _finish_small(smt, lax.broadcasted_iota(_i32, smt.shape, 0))


def _proj_call(x2, g, w, wvt, wst, cos_t, sa_t, sb_t, cw, cb, bias, biast, *, seq):
    tokens = x2.shape[0]
    rows = min(PROJ_ROWS, seq)
    tiles_per_seq = seq // rows
    grid = (tokens // rows,)
    const = lambda i: (0, 0)
    tok = lambda i: (i, 0)
    tok_t = lambda i: (0, i)
    pos = lambda i: (i % tiles_per_seq, 0)
    in_specs = [
        pl.BlockSpec((rows, D_MODEL), tok),
        pl.BlockSpec((1, D_MODEL), const),
        pl.BlockSpec((D_MODEL, _C_END), const),
        pl.BlockSpec((A_WIDTH, D_MODEL), const),
        pl.BlockSpec((SMALL_ROWS, D_MODEL), const),
        pl.BlockSpec((rows, LANES), pos),
        pl.BlockSpec((rows, LANES), pos),
        pl.BlockSpec((rows, LANES), pos),
        pl.BlockSpec((CONV_WIDTH, 2 * B_WIDTH), const),
        pl.BlockSpec((1, 2 * B_WIDTH), const),
        pl.BlockSpec((1, LANES), const),
        pl.BlockSpec((SMALL_ROWS, LANES), const),
    ]
    out_shape = [
        jax.ShapeDtypeStruct((tokens, A_WIDTH), _bf16),
        jax.ShapeDtypeStruct((tokens, A_WIDTH), _bf16),
        jax.ShapeDtypeStruct((A_WIDTH, tokens), _bf16),
        jax.ShapeDtypeStruct((tokens, IDX_WIDTH), _bf16),
        jax.ShapeDtypeStruct((tokens, LANES), _bf16),
        jax.ShapeDtypeStruct((tokens, B_WIDTH), _bf16),
        jax.ShapeDtypeStruct((tokens, B_WIDTH), _bf16),
        jax.ShapeDtypeStruct((tokens, B_WIDTH), _bf16),
        jax.ShapeDtypeStruct((tokens, LANES), _f32),
        jax.ShapeDtypeStruct((SMALL_ROWS, tokens), _f32),
    ]
    out_specs = [
        pl.BlockSpec((rows, A_WIDTH), tok),
        pl.BlockSpec((rows, A_WIDTH), tok),
        pl.BlockSpec((A_WIDTH, rows), tok_t),
        pl.BlockSpec((rows, IDX_WIDTH), tok),
        pl.BlockSpec((rows, LANES), tok),
        pl.BlockSpec((rows, B_WIDTH), tok),
        pl.BlockSpec((rows, B_WIDTH), tok),
        pl.BlockSpec((rows, B_WIDTH), tok),
        pl.BlockSpec((rows, LANES), tok),
        pl.BlockSpec((SMALL_ROWS, rows), tok_t),
    ]
    return pl.pallas_call(
        functools.partial(_proj_kernel, tiles_per_seq=tiles_per_seq),
        grid=grid, in_specs=in_specs, out_specs=out_specs, out_shape=out_shape,
        scratch_shapes=[pltpu.VMEM((rows + 2 * CONV_HALO, 2 * B_WIDTH), _f32)],
        compiler_params=_params(1), name="proj",
    )(x2, g, w, wvt, wst, cos_t, sa_t, sb_t, cw, cb, bias, biast)


def _sortable(x):
    bits = pltpu.bitcast(x, _i32)
    return jnp.where(bits >= 0, bits, bits ^ jnp.int32(0x7FFFFFFF))


def _dsa_kernel(qi_ref, qa_ref, wt_ref, ki2_ref, ka_ref, vat_ref, o_ref,
                skey, half, qim, qam, bias_sc, lg_sc, m_sc, l_sc, acc_sc, *, k_sel, ts):
    qt = pl.program_id(1)
    tq = qi_ref.shape[0]
    q0 = qt * tq
    n_kt = (q0 + tq + ts - 1) // ts

    lo_half = lax.broadcasted_iota(_i32, (tq, LANES), 1) < (LANES // 2)
    zero = jnp.zeros((tq, LANES), _bf16)
    for p in range(IDX_HEADS // 2):
        pair = qi_ref[:, p * LANES:(p + 1) * LANES]
        qim[2 * p] = jnp.where(lo_half, pair, zero)
        qim[2 * p + 1] = jnp.where(lo_half, zero, pair)
    for p in range(A_HEADS // 2):
        pair = qa_ref[:, p * LANES:(p + 1) * LANES]
        qam[2 * p] = jnp.where(lo_half, pair, zero)
        qam[2 * p + 1] = jnp.where(lo_half, zero, pair)

    qchunk = (q0 + lax.broadcasted_iota(_i32, (ts, tq), 1)) // CHUNK
    krow = lax.broadcasted_iota(_i32, (ts, tq), 0)

    def score_body(kt, carry):
        k0 = pl.multiple_of(kt * ts, ts)
        kslab = ki2_ref[pl.ds(k0, ts), :]
        acc = jnp.zeros((ts, tq), _f32)
        for hh in range(IDX_HEADS):
            s = _dot_nt(kslab, qim[hh])
            acc = acc + jnp.maximum(s, 0.0) * wt_ref[_S_W + hh:_S_W + hh + 1, :]
        admissible = (k0 + krow) // CHUNK <= qchunk
        key = jnp.where(admissible, _sortable(acc), jnp.int32(INT_MIN))
        skey[pl.ds(k0, ts), :] = key
        half[pl.ds(k0, ts), :] = (key >> HALF_BITS).astype(_i16)
        return carry

    lax.fori_loop(0, n_kt, score_body, 0)

    def search16(need):
        def bit_body(i, carry):
            t, above = carry
            cand = t + jnp.left_shift(jnp.int32(1), HALF_BITS - 1 - i)
            cand16 = cand.astype(_i16)

            def count_body(kt, cnt):
                k0 = pl.multiple_of(kt * ts, ts)
                ge = jnp.where(half[pl.ds(k0, ts), :] >= cand16, _bf16(1), _bf16(0))
                parts = [ge[r:r + PACKED_SUBLANES] for r in range(0, ts, PACKED_SUBLANES)]
                while len(parts) > 1:
                    parts = [a + b for a, b in zip(parts[0::2], parts[1::2])]
                return cnt + parts[0].astype(_f32)

            cnt = lax.fori_loop(0, n_kt, count_body, jnp.zeros((PACKED_SUBLANES, tq), _f32))
            total = cnt.sum(axis=0, keepdims=True)
            ok = total >= need
            return jnp.where(ok, cand, t), jnp.where(ok, above, total)

        return lax.fori_loop(0, HALF_BITS, bit_body,
                             (jnp.full((1, tq), HALF_MIN, _i32), jnp.zeros((1, tq), _f32)))

    thr_hi, above = search16(jnp.float32(k_sel))

    def low_body(kt, carry):
        k0 = pl.multiple_of(kt * ts, ts)
        key = skey[pl.ds(k0, ts), :]
        low = (key & jnp.int32(HALF_MASK)) + jnp.int32(HALF_MIN)
        half[pl.ds(k0, ts), :] = jnp.where((key >> HALF_BITS) == thr_hi, low,
                                           jnp.int32(HALF_MIN)).astype(_i16)
        return carry

    lax.fori_loop(0, n_kt, low_body, 0)
    thr_lo, _ = search16(jnp.float32(k_sel) - above)
    thr = jnp.left_shift(thr_hi, HALF_BITS) + (thr_lo - jnp.int32(HALF_MIN))
    thr = jnp.maximum(thr, jnp.int32(INT_MIN + 1))

    m_sc[...] = jnp.full(m_sc.shape, NEG_BIG, _f32)
    l_sc[...] = jnp.zeros(l_sc.shape, _f32)
    acc_sc[...] = jnp.zeros(acc_sc.shape, _f32)

    def att_body(kt, carry):
        k0 = pl.multiple_of(kt * ts, ts)
        bias_sc[...] = jnp.where(skey[pl.ds(k0, ts), :] >= thr, 0.0, -jnp.inf)

        def logits(hh):
            p = hh // 2
            kpair = ka_ref[pl.ds(k0, ts), p * LANES:(p + 1) * LANES]
            lg = _dot_nt(kpair, qam[hh]) + bias_sc[...]
            lg_sc[hh % 2] = lg
            return lg.max(axis=0, keepdims=True)

        col_max = logits(0)
        for hh in range(A_HEADS):
            next_max = logits(hh + 1) if hh + 1 < A_HEADS else None
            m_old = m_sc[hh:hh + 1, :]
            m_new = jnp.maximum(m_old, col_max)
            alpha = jnp.exp2(m_old - m_new)
            pr = jnp.exp2(lg_sc[hh % 2] - m_new)
            l_sc[hh:hh + 1, :] = alpha * l_sc[hh:hh + 1, :] + pr.sum(axis=0, keepdims=True)
            rows = slice(hh * A_HEAD_DIM, (hh + 1) * A_HEAD_DIM)
            pv = _dot(vat_ref[rows, pl.ds(k0, ts)], pr.astype(_bf16))
            acc_sc[rows, :] = alpha * acc_sc[rows, :] + pv
            m_sc[hh:hh + 1, :] = m_new
            col_max = next_max
        return carry

    lax.fori_loop(0, n_kt, att_body, 0)

    outs = []
    for hh in range(A_HEADS):
        rows = slice(hh * A_HEAD_DIM, (hh + 1) * A_HEAD_DIM)
        outs.append(acc_sc[rows, :] / l_sc[hh:hh + 1, :])
    o_ref[...] = jnp.concatenate(outs, axis=0).T


def _dsa_call(qi, qa, smallt, ki2, ka, vat, *, batch, seq, k_sel):
    tq = min(DSA_TQ, seq)
    ts = min(DSA_TS, seq)
    nq = seq // tq
    qtok = lambda b, q: (b * nq + q, 0)
    return pl.pallas_call(
        functools.partial(_dsa_kernel, k_sel=k_sel, ts=ts),
        grid=(batch, nq),
        in_specs=[
            pl.BlockSpec((tq, IDX_WIDTH), qtok),
            pl.BlockSpec((tq, A_WIDTH), qtok),
            pl.BlockSpec((SMALL_ROWS, tq), lambda b, q: (0, b * nq + q)),
            pl.BlockSpec((seq, LANES), lambda b, q: (b, 0)),
            pl.BlockSpec((seq, A_WIDTH), lambda b, q: (b, 0)),
            pl.BlockSpec((A_WIDTH, seq), lambda b, q: (0, b)),
        ],
        out_specs=pl.BlockSpec((tq, A_WIDTH), qtok),
        out_shape=jax.ShapeDtypeStruct((batch * seq, A_WIDTH), _f32),
        scratch_shapes=[
            pltpu.VMEM((seq, tq), _i32),
            pltpu.VMEM((seq, tq), _i16),
            pltpu.VMEM((IDX_HEADS, tq, LANES), _bf16),
            pltpu.VMEM((A_HEADS, tq, LANES), _bf16),
            pltpu.VMEM((ts, tq), _f32),
            pltpu.VMEM((2, ts, tq), _f32),
            pltpu.VMEM((A_HEADS, tq), _f32),
            pltpu.VMEM((A_HEADS, tq), _f32),
            pltpu.VMEM((A_WIDTH, tq), _f32),
        ],
        compiler_params=_params(2), name="dsa",
    )(qi, qa, smallt, ki2, ka, vat)


def _mlstm_kernel(q_ref, k_ref, v_ref, small_ref, smallt_ref, o_ref, c_sc, m_sc):
    rows = q_ref.shape[0]
    L = min(MLSTM_L, rows)
    d = B_HEAD_DIM

    @pl.when(pl.program_id(1) == 0)
    def _():
        c_sc[...] = jnp.zeros(c_sc.shape, _f32)
        m_sc[...] = jnp.zeros(m_sc.shape, _f32)

    tril = lax.broadcasted_iota(_i32, (L, L), 1) <= lax.broadcasted_iota(_i32, (L, L), 0)
    trilb = jnp.where(tril, 1.0, 0.0).astype(_bf16)
    lane0 = lax.broadcasted_iota(_i32, (L, d), 1) == 0

    def chunk_body(c, carry):
        r0 = pl.multiple_of(c * L, L)
        sm = small_ref[pl.ds(r0, L), :]
        smt = smallt_ref[:, pl.ds(r0, L)]
        bcol = sum(_dot(trilb, part) for part in _split3(sm))
        brow = sum(_dot_nt(part, trilb) for part in _split3(smt))
        for hh in range(B_HEADS):
            cols = slice(hh * d, (hh + 1) * d)
            li_c, b_c = sm[:, _S_LI + hh:_S_LI + hh + 1], bcol[:, _S_LF + hh:_S_LF + hh + 1]
            li_r, b_r = smt[_S_LI + hh:_S_LI + hh + 1, :], brow[_S_LF + hh:_S_LF + hh + 1, :]
            m_prev = m_sc[hh:hh + 1, 0:1]
            dmat = jnp.where(tril, b_c - b_r + li_r, -jnp.inf)
            m_inter = b_c + m_prev
            m_t = jnp.maximum(m_inter, dmat.max(axis=1, keepdims=True))
            w_inter = jnp.exp(m_inter - m_t)
            qh = q_ref[pl.ds(r0, L), cols]
            kh = k_ref[pl.ds(r0, L), cols]
            vh = v_ref[pl.ds(r0, L), cols]
            s_qk = _dot_nt(qh, kh) * jnp.exp(dmat - m_t)
            cext = c_sc[hh]
            inter = _dot(qh, cext.astype(_bf16))
            num = w_inter * inter[:, :d] + _dot(s_qk.astype(_bf16), vh)
            den = w_inter * inter[:, d:d + 1] + s_qk.sum(axis=1, keepdims=True)
            o_ref[pl.ds(r0, L), cols] = num / jnp.maximum(jnp.abs(den), jnp.exp(-m_t))
            b_last = b_c[L - 1:L, :]
            g_c = b_last - b_c + li_c
            m_new = jnp.maximum(b_last + m_prev, g_c.max(axis=0, keepdims=True))
            decay = jnp.exp(b_last + m_prev - m_new)
            wk = jnp.exp(g_c - m_new)
            wkv = jnp.concatenate([wk * vh.astype(_f32), jnp.where(lane0, wk, 0.0)], axis=1)
            c_sc[hh] = decay * cext + _dot_tn(kh, wkv.astype(_bf16))
            m_sc[hh:hh + 1, :] = jnp.broadcast_to(m_new, (1, LANES))
        return carry

    lax.fori_loop(0, rows // L, chunk_body, 0)


def _mlstm_call(qb, kb, vb, small, smallt, *, batch, seq):
    rows = min(MLSTM_ROWS, seq)
    nb = seq // rows
    tok = lambda b, j: (b * nb + j, 0)
    return pl.pallas_call(
        _mlstm_kernel,
        grid=(batch, nb),
        in_specs=[
            pl.BlockSpec((rows, B_WIDTH), tok),
            pl.BlockSpec((rows, B_WIDTH), tok),
            pl.BlockSpec((rows, B_WIDTH), tok),
            pl.BlockSpec((rows, LANES), tok),
            pl.BlockSpec((SMALL_ROWS, rows), lambda b, j: (0, b * nb + j)),
        ],
        out_specs=pl.BlockSpec((rows, B_WIDTH), tok),
        out_shape=jax.ShapeDtypeStruct((batch * seq, B_WIDTH), _f32),
        scratch_shapes=[
            pltpu.VMEM((B_HEADS, B_HEAD_DIM, 2 * B_HEAD_DIM), _f32),
            pltpu.VMEM((SUBLANES, LANES), _f32),
        ],
        compiler_params=_params(2), name="mlstm",
    )(qb, kb, vb, small, smallt)


_G_AZ, _G_BZ, _G_BO = 0, A_WIDTH, A_WIDTH + B_WIDTH
_G_GA = _G_BO + B_WIDTH
_G_GB = _G_GA + D_MODEL
_G_END = _G_GB + D_MODEL


def _out_kernel(x_ref, oa_ref, cell_ref, g_ref, wg_ref, wba_ref, wbb_ref, wo_ref, hng_ref, fg_ref,
                out_ref, *, final_norm):
    x = x_ref[...]
    h = _rmsnorm(x, g_ref[...]).astype(_bf16)
    a_z = _dot(h, wg_ref[:, _G_AZ:_G_AZ + A_WIDTH])
    y_a = _dot((oa_ref[...] * _silu(a_z)).astype(_bf16), wba_ref[...])

    b_o = _dot(h, wg_ref[:, _G_BO:_G_BO + B_WIDTH])
    t = jax.nn.sigmoid(b_o) * cell_ref[...]
    normed = []
    for hh in range(B_HEADS):
        th = t[:, hh * B_HEAD_DIM:(hh + 1) * B_HEAD_DIM]
        mu = jnp.mean(th, axis=-1, keepdims=True)
        var = jnp.mean(jnp.square(th - mu), axis=-1, keepdims=True)
        normed.append((th - mu) * lax.rsqrt(var + EPS))
    b_z = _dot(h, wg_ref[:, _G_BZ:_G_BZ + B_WIDTH])
    h_b = jnp.concatenate(normed, axis=1) * hng_ref[...] * _silu(b_z)
    y_b = _dot(h_b.astype(_bf16), wbb_ref[...])

    g_a = _dot(h, wg_ref[:, _G_GA:_G_GA + D_MODEL])
    g_b = _dot(h, wg_ref[:, _G_GB:_G_GB + D_MODEL])
    merged = jax.nn.sigmoid(g_a) * y_a + jax.nn.sigmoid(g_b) * y_b
    y = x + _dot(merged.astype(_bf16), wo_ref[...])
    if final_norm:
        y = _rmsnorm(y, fg_ref[...])
    out_ref[...] = y


def _out_call(x2, oa, cell, g, wg, wba, wbb, wo, hng, fg, *, final_norm):
    tokens = x2.shape[0]
    rows = min(OUT_ROWS, tokens)
    const = lambda i: (0, 0)
    tok = lambda i: (i, 0)
    return pl.pallas_call(
        functools.partial(_out_kernel, final_norm=final_norm),
        grid=(tokens // rows,),
        in_specs=[
            pl.BlockSpec((rows, D_MODEL), tok),
            pl.BlockSpec((rows, A_WIDTH), tok),
            pl.BlockSpec((rows, B_WIDTH), tok),
            pl.BlockSpec((1, D_MODEL), const),
            pl.BlockSpec((D_MODEL, _G_END), const),
            pl.BlockSpec((A_WIDTH, D_MODEL), const),
            pl.BlockSpec((B_WIDTH, D_MODEL), const),
            pl.BlockSpec((D_MODEL, D_MODEL), const),
            pl.BlockSpec((1, B_WIDTH), const),
            pl.BlockSpec((1, D_MODEL), const),
        ],
        out_specs=pl.BlockSpec((rows, D_MODEL), tok),
        out_shape=jax.ShapeDtypeStruct((tokens, D_MODEL), _f32),
        compiler_params=_params(1), name="out",
    )(x2, oa, cell, g, wg, wba, wbb, wo, hng, fg)


def _rope_tables(seq):
    inv = ROPE_THETA ** (-(jnp.arange(ROPE_HALF, dtype=_f32) * 2.0 / (2 * ROPE_HALF)))
    ang = jnp.arange(seq).astype(_f32)[:, None] * inv[None, :]
    cos, sin = jnp.cos(ang), jnp.sin(ang)
    pad = jnp.zeros((seq, A_HEAD_DIM - 2 * ROPE_HALF), _f32)
    zero = jnp.zeros((seq, ROPE_HALF), _f32)
    cos_h = jnp.concatenate([cos, cos, pad + 1.0], axis=1)
    sa_h = jnp.concatenate([-sin, zero, pad], axis=1)
    sb_h = jnp.concatenate([zero, sin, pad], axis=1)
    rep = LANES // A_HEAD_DIM
    return tuple(jnp.tile(t, (1, rep)) for t in (cos_h, sa_h, sb_h))


def kernel(x, norm_g, w_in, conv_w, conv_b, b_igate, b_fgate, head_norm_g,
           w_branch_a, w_branch_b, w_out, final_norm_g):
    batch, seq, _ = x.shape
    depth = w_in.shape[0]
    k_sel = min(TOPK_MAX, seq // 4)
    offsets = np.cumsum(IN_SIZES)[:-1]
    cos_t, sa_t, sb_t = _rope_tables(seq)
    x2 = x.reshape(batch * seq, D_MODEL)
    for l in range(depth):
        (a_q, a_k, a_v, a_z, i_q, i_k, i_w, b_q, b_k, b_v, b_z, b_o, b_i, b_f, g_a, g_b) = \
            jnp.split(w_in[l], offsets, axis=-1)
        small_w = jnp.concatenate(
            [b_i, b_f, i_w, jnp.zeros((D_MODEL, LANES - 2 * B_HEADS - IDX_HEADS), _f32)], axis=1)
        w_proj = jnp.concatenate([a_q, a_k, i_q, b_q, b_k, b_v, i_k, i_k, small_w], axis=1).astype(_bf16)
        wvt = a_v.T.astype(_bf16)
        wst = small_w[:, :SMALL_ROWS].T.astype(_bf16)
        bias = jnp.concatenate(
            [b_igate[l], b_fgate[l], jnp.zeros((LANES - 2 * B_HEADS,), _f32)])
        biast = jnp.broadcast_to(bias[:SMALL_ROWS, None], (SMALL_ROWS, LANES))
        w_gate = jnp.concatenate([a_z, b_z, b_o, g_a, g_b], axis=1).astype(_bf16)
        g = norm_g[l].reshape(1, D_MODEL)

        qa, ka, vat, qi, ki2, qb, kb, vb, small, smallt = _proj_call(
            x2, g, w_proj, wvt, wst, cos_t, sa_t, sb_t, conv_w[l], conv_b[l].reshape(1, -1),
            bias.reshape(1, LANES), biast, seq=seq)
        o_a = _dsa_call(qi, qa, smallt, ki2, ka, vat, batch=batch, seq=seq, k_sel=k_sel)
        cell = _mlstm_call(qb, kb, vb, small, smallt, batch=batch, seq=seq)
        x2 = _out_call(x2, o_a, cell, g, w_gate, w_branch_a[l].astype(_bf16),
                       w_branch_b[l].astype(_bf16), w_out[l].astype(_bf16),
                       head_norm_g[l].reshape(1, B_WIDTH), final_norm_g.reshape(1, D_MODEL),
                       final_norm=(l == depth - 1))
    return x2.reshape(batch, seq, D_MODEL)
```

```python
import functools

import jax
import jax.numpy as jnp
import numpy as np
from jax import lax
from jax.experimental import pallas as pl
from jax.experimental.pallas import tpu as pltpu

D_MODEL = 1024
CHUNK = 64
ROPE_THETA = 500000.0
EPS = 1e-6
A_HEADS, A_HEAD_DIM = 8, 64
A_WIDTH = A_HEADS * A_HEAD_DIM
IDX_HEADS, IDX_DIM = 16, 64
IDX_WIDTH = IDX_HEADS * IDX_DIM
TOPK_MAX = 256
B_HEADS, B_HEAD_DIM = 4, 128
B_WIDTH = B_HEADS * B_HEAD_DIM
CONV_WIDTH = 4
IN_SIZES = (A_WIDTH,) * 4 + (IDX_WIDTH, IDX_DIM, IDX_HEADS) + (B_WIDTH,) * 5 \
    + (B_HEADS, B_HEADS, D_MODEL, D_MODEL)

LANES = 128
SUBLANES = 8
VMEM_LIMIT_BYTES = 56 * 1024 * 1024

PROJ_ROWS = 512
CONV_HALO = SUBLANES
DSA_TQ = 256
DSA_TS = 512
MLSTM_L = 128
MLSTM_ROWS = 1024
OUT_ROWS = 256
ROPE_HALF = A_HEAD_DIM // 8
SMALL_ROWS = 32
INT_MIN = -2 ** 31
HALF_BITS = 16
HALF_MIN = -2 ** (HALF_BITS - 1)
HALF_MASK = 2 ** HALF_BITS - 1
PACKED_SUBLANES = 2 * SUBLANES
LOG2E = 1.4426950408889634
NEG_BIG = -1e30
V_ROWS = A_HEAD_DIM + PACKED_SUBLANES
SHIFT_SLACK = 1.02
SHIFT_CAP = 40.0

assert CONV_HALO >= CONV_WIDTH - 1 and IDX_DIM == A_HEAD_DIM == LANES // 2 and B_HEAD_DIM == LANES

_f32 = jnp.float32
_bf16 = jnp.bfloat16
_i32 = jnp.int32
_i16 = jnp.int16


def _dot(a, b):
    return jnp.dot(a, b, preferred_element_type=_f32)


def _dot_nt(a, b):
    return lax.dot_general(a, b, (((1,), (1,)), ((), ())), preferred_element_type=_f32)


def _dot_tn(a, b):
    return lax.dot_general(a, b, (((0,), (0,)), ((), ())), preferred_element_type=_f32)


def _split3(x):
    hi = x.astype(_bf16)
    r1 = x - hi.astype(_f32)
    mid = r1.astype(_bf16)
    lo = (r1 - mid.astype(_f32)).astype(_bf16)
    return hi, mid, lo


def _rmsnorm(x, g):
    ms = jnp.mean(x * x, axis=-1, keepdims=True)
    return x * lax.rsqrt(ms + EPS) * g


def _silu(x):
    return x * jax.nn.sigmoid(x)


def _params(n_axes):
    return pltpu.CompilerParams(dimension_semantics=("arbitrary",) * n_axes,
                                vmem_limit_bytes=VMEM_LIMIT_BYTES)


_C_AQ = 0
_C_AK = _C_AQ + A_WIDTH
_C_IQ = _C_AK + A_WIDTH
_C_BQK = _C_IQ + IDX_WIDTH
_C_BV = _C_BQK + 2 * B_WIDTH
_C_IK2 = _C_BV + B_WIDTH
_C_SMALL = _C_IK2 + LANES
_C_END = _C_SMALL + LANES
_S_LI, _S_LF, _S_W = 0, B_HEADS, 2 * B_HEADS


def _rope(x, cos_t, sa_t, sb_t):
    outs = []
    for g in range(x.shape[1] // LANES):
        xs = x[:, g * LANES:(g + 1) * LANES]
        up = pltpu.roll(xs, LANES - ROPE_HALF, 1)
        dn = pltpu.roll(xs, ROPE_HALF, 1)
        outs.append(xs * cos_t + up * sa_t + dn * sb_t)
    return outs[0] if len(outs) == 1 else jnp.concatenate(outs, axis=1)


def _finish_small(t, idx):
    lf = jax.nn.log_sigmoid(t)
    return jnp.where((idx >= _S_LF) & (idx < _S_W), lf,
                     jnp.where(idx >= _S_W, t * (IDX_HEADS ** -0.5), t))


def _proj_kernel(x_ref, g_ref, w_ref, wvt_ref, wst_ref, cos_ref, sa_ref, sb_ref, cw_ref, cb_ref,
                 bias_ref, biast_ref,
                 qa_ref, ka_ref, vat_ref, qi_ref, ki2_ref, qb_ref, kb_ref, vb_ref,
                 small_ref, smallt_ref, ubuf, *, tiles_per_seq):
    i = pl.program_id(0)
    rows = x_ref.shape[0]
    h = _rmsnorm(x_ref[...], g_ref[...]).astype(_bf16)
    cos_t, sa_t, sb_t = cos_ref[...], sa_ref[...], sb_ref[...]

    qa = _rope(_dot(h, w_ref[:, _C_AQ:_C_AQ + A_WIDTH]), cos_t, sa_t, sb_t)
    qa_ref[...] = (qa * (A_HEAD_DIM ** -0.5 * LOG2E)).astype(_bf16)
    ka_ref[...] = _rope(_dot(h, w_ref[:, _C_AK:_C_AK + A_WIDTH]), cos_t, sa_t, sb_t).astype(_bf16)
    vat = _dot_nt(wvt_ref[...], h).reshape(A_HEADS, A_HEAD_DIM, rows)
    tail = jnp.where(lax.broadcasted_iota(_i32, (A_HEADS, PACKED_SUBLANES, rows), 1) == 0, 1.0, 0.0)
    vat_ref[...] = jnp.concatenate([vat, tail], axis=1).reshape(A_HEADS * V_ROWS, rows).astype(_bf16)
    qi = _rope(_dot(h, w_ref[:, _C_IQ:_C_IQ + IDX_WIDTH]), cos_t, sa_t, sb_t)
    qi_ref[...] = (qi * (IDX_DIM ** -0.5)).astype(_bf16)
    ki2_ref[...] = _rope(_dot(h, w_ref[:, _C_IK2:_C_IK2 + LANES]), cos_t, sa_t, sb_t).astype(_bf16)

    @pl.when(i % tiles_per_seq == 0)
    def _():
        ubuf[0:CONV_HALO, :] = jnp.zeros((CONV_HALO, 2 * B_WIDTH), _f32)

    ubuf[CONV_HALO:CONV_HALO + rows, :] = _dot(h, w_ref[:, _C_BQK:_C_BQK + 2 * B_WIDTH])
    conv = None
    for j in range(CONV_WIDTH):
        off = CONV_HALO - (CONV_WIDTH - 1) + j
        term = ubuf[off:off + rows, :] * cw_ref[j:j + 1, :]
        conv = term if conv is None else conv + term
    conv = conv + cb_ref[...]
    ubuf[0:CONV_HALO, :] = ubuf[rows:rows + CONV_HALO, :]
    qk = _silu(conv)
    qb_ref[...] = qk[:, :B_WIDTH].astype(_bf16)
    kb_ref[...] = (qk[:, B_WIDTH:] * (B_HEAD_DIM ** -0.5)).astype(_bf16)
    vb_ref[...] = _dot(h, w_ref[:, _C_BV:_C_BV + B_WIDTH]).astype(_bf16)

    sm = _dot(h, w_ref[:, _C_SMALL:_C_SMALL + LANES]) + bias_ref[...]
    small_ref[...] = _finish_small(sm, lax.broadcasted_iota(_i32, sm.shape, 1))
    smt = _dot_nt(wst_ref[...], h) + biast_ref[:, 0:1]
    smallt_ref[...] = _finish_small(smt, lax.broadcasted_iota(_i32, smt.shape, 0))


def _proj_call(x2, g, w, wvt, wst, cos_t, sa_t, sb_t, cw, cb, bias, biast, *, seq):
    tokens = x2.shape[0]
    rows = min(PROJ_ROWS, seq)
    tiles_per_seq = seq // rows
    grid = (tokens // rows,)
    const = lambda i: (0, 0)
    tok = lambda i: (i, 0)
    tok_t = lambda i: (0, i)
    pos = lambda i: (i % tiles_per_seq, 0)
    in_specs = [
        pl.BlockSpec((rows, D_MODEL), tok),
        pl.BlockSpec((1, D_MODEL), const),
        pl.BlockSpec((D_MODEL, _C_END), const),
        pl.BlockSpec((A_WIDTH, D_MODEL), const),
        pl.BlockSpec((SMALL_ROWS, D_MODEL), const),
        pl.BlockSpec((rows, LANES), pos),
        pl.BlockSpec((rows, LANES), pos),
        pl.BlockSpec((rows, LANES), pos),
        pl.BlockSpec((CONV_WIDTH, 2 * B_WIDTH), const),
        pl.BlockSpec((1, 2 * B_WIDTH), const),
        pl.BlockSpec((1, LANES), const),
        pl.BlockSpec((SMALL_ROWS, LANES), const),
    ]
    out_shape = [
        jax.ShapeDtypeStruct((tokens, A_WIDTH), _bf16),
        jax.ShapeDtypeStruct((tokens, A_WIDTH), _bf16),
        jax.ShapeDtypeStruct((A_HEADS * V_ROWS, tokens), _bf16),
        jax.ShapeDtypeStruct((tokens, IDX_WIDTH), _bf16),
        jax.ShapeDtypeStruct((tokens, LANES), _bf16),
        jax.ShapeDtypeStruct((tokens, B_WIDTH), _bf16),
        jax.ShapeDtypeStruct((tokens, B_WIDTH), _bf16),
        jax.ShapeDtypeStruct((tokens, B_WIDTH), _bf16),
        jax.ShapeDtypeStruct((tokens, LANES), _f32),
        jax.ShapeDtypeStruct((SMALL_ROWS, tokens), _f32),
    ]
    out_specs = [
        pl.BlockSpec((rows, A_WIDTH), tok),
        pl.BlockSpec((rows, A_WIDTH), tok),
        pl.BlockSpec((A_HEADS * V_ROWS, rows), tok_t),
        pl.BlockSpec((rows, IDX_WIDTH), tok),
        pl.BlockSpec((rows, LANES), tok),
        pl.BlockSpec((rows, B_WIDTH), tok),
        pl.BlockSpec((rows, B_WIDTH), tok),
        pl.BlockSpec((rows, B_WIDTH), tok),
        pl.BlockSpec((rows, LANES), tok),
        pl.BlockSpec((SMALL_ROWS, rows), tok_t),
    ]
    return pl.pallas_call(
        functools.partial(_proj_kernel, tiles_per_seq=tiles_per_seq),
        grid=grid, in_specs=in_specs, out_specs=out_specs, out_shape=out_shape,
        scratch_shapes=[pltpu.VMEM((rows + 2 * CONV_HALO, 2 * B_WIDTH), _f32)],
        compiler_params=_params(1), name="proj",
    )(x2, g, w, wvt, wst, cos_t, sa_t, sb_t, cw, cb, bias, biast)


def _sortable(x):
    bits = pltpu.bitcast(x, _i32)
    return jnp.where(bits >= 0, bits, bits ^ jnp.int32(0x7FFFFFFF))


def _head_halves(pair_f32, lane):
    lo = lane < A_HEAD_DIM
    return (jnp.where(lo, pair_f32, 0.0),
            jnp.where(lo, pltpu.roll(pair_f32, A_HEAD_DIM, 1), 0.0))


def _dsa_kernel(qi_ref, qa_ref, wt_ref, ki2_ref, ka_ref, vat_ref, o_ref,
                skey, half, qim, qe, kext, kmax_sc, bias_sc, lg_sc, m_sc, acc_sc, *, k_sel, ts):
    qt = pl.program_id(1)
    tq = qi_ref.shape[0]
    seq = ka_ref.shape[0]
    q0 = qt * tq
    n_kt = (q0 + tq + ts - 1) // ts

    @pl.when(qt == 0)
    def _():
        klane = lax.broadcasted_iota(_i32, (ts, LANES), 1)
        one_hot = jnp.where(klane == A_HEAD_DIM, 1.0, 0.0)
        kmax_sc[...] = jnp.zeros(kmax_sc.shape, _f32)

        def kext_body(kt, carry):
            k0 = pl.multiple_of(kt * ts, ts)
            for p in range(A_HEADS // 2):
                pair = ka_ref[pl.ds(k0, ts), p * LANES:(p + 1) * LANES].astype(_f32)
                for j, kh in enumerate(_head_halves(pair, klane)):
                    hh = 2 * p + j
                    kext[pl.ds(k0, ts), hh * LANES:(hh + 1) * LANES] = (kh + one_hot).astype(_bf16)
                    n2 = jnp.sum(kh * kh, axis=1, keepdims=True).max(axis=0, keepdims=True)
                    kmax_sc[hh:hh + 1, :] = jnp.maximum(kmax_sc[hh:hh + 1, :], n2)
            return carry

        lax.fori_loop(0, seq // ts, kext_body, 0)

    qlane = lax.broadcasted_iota(_i32, (tq, LANES), 1)
    lo_half = qlane < (LANES // 2)
    zero = jnp.zeros((tq, LANES), _bf16)
    for p in range(IDX_HEADS // 2):
        pair = qi_ref[:, p * LANES:(p + 1) * LANES]
        qim[2 * p] = jnp.where(lo_half, pair, zero)
        qim[2 * p + 1] = jnp.where(lo_half, zero, pair)
    ones_sq = jnp.ones((LANES, LANES), _bf16)
    worst = jnp.zeros((tq, LANES), _f32)
    for p in range(A_HEADS // 2):
        pair = qa_ref[:, p * LANES:(p + 1) * LANES].astype(_f32)
        for j, qh in enumerate(_head_halves(pair, qlane)):
            hh = 2 * p + j
            q2 = _dot((qh * qh).astype(_bf16), ones_sq)
            prod = jnp.maximum(q2 * kmax_sc[hh:hh + 1, :], 1e-30)
            shift = prod * lax.rsqrt(prod) * SHIFT_SLACK
            qe[hh] = jnp.where(qlane == A_HEAD_DIM, -shift, qh).astype(_bf16)
            worst = jnp.maximum(worst, shift)
    worst_shift = jnp.max(worst)

    qchunk = (q0 + lax.broadcasted_iota(_i32, (ts, tq), 1)) // CHUNK
    krow = lax.broadcasted_iota(_i32, (ts, tq), 0)

    def score_body(kt, carry):
        k0 = pl.multiple_of(kt * ts, ts)
        kslab = ki2_ref[pl.ds(k0, ts), :]
        acc = jnp.zeros((ts, tq), _f32)
        for hh in range(IDX_HEADS):
            s = _dot_nt(kslab, qim[hh])
            acc = acc + jnp.maximum(s, 0.0) * wt_ref[_S_W + hh:_S_W + hh + 1, :]
        admissible = (k0 + krow) // CHUNK <= qchunk
        key = jnp.where(admissible, _sortable(acc), jnp.int32(INT_MIN))
        skey[pl.ds(k0, ts), :] = key
        half[pl.ds(k0, ts), :] = (key >> HALF_BITS).astype(_i16)
        return carry

    lax.fori_loop(0, n_kt, score_body, 0)

    def search16(need):
        def bit_body(i, carry):
            t, above = carry
            cand = t + jnp.left_shift(jnp.int32(1), HALF_BITS - 1 - i)
            cand16 = cand.astype(_i16)

            def count_body(kt, cnt):
                k0 = pl.multiple_of(kt * ts, ts)
                ge = jnp.where(half[pl.ds(k0, ts), :] >= cand16, _bf16(1), _bf16(0))
                parts = [ge[r:r + PACKED_SUBLANES] for r in range(0, ts, PACKED_SUBLANES)]
                while len(parts) > 1:
                    parts = [a + b for a, b in zip(parts[0::2], parts[1::2])]
                return cnt + parts[0].astype(_f32)

            cnt = lax.fori_loop(0, n_kt, count_body, jnp.zeros((PACKED_SUBLANES, tq), _f32))
            total = cnt.sum(axis=0, keepdims=True)
            ok = total >= need
            return jnp.where(ok, cand, t), jnp.where(ok, above, total)

        return lax.fori_loop(0, HALF_BITS, bit_body,
                             (jnp.full((1, tq), HALF_MIN, _i32), jnp.zeros((1, tq), _f32)))

    thr_hi, above = search16(jnp.float32(k_sel))

    def low_body(kt, carry):
        k0 = pl.multiple_of(kt * ts, ts)
        key = skey[pl.ds(k0, ts), :]
        low = (key & jnp.int32(HALF_MASK)) + jnp.int32(HALF_MIN)
        half[pl.ds(k0, ts), :] = jnp.where((key >> HALF_BITS) == thr_hi, low,
                                           jnp.int32(HALF_MIN)).astype(_i16)
        return carry

    lax.fori_loop(0, n_kt, low_body, 0)
    thr_lo, _ = search16(jnp.float32(k_sel) - above)
    thr = jnp.left_shift(thr_hi, HALF_BITS) + (thr_lo - jnp.int32(HALF_MIN))
    thr = jnp.maximum(thr, jnp.int32(INT_MIN + 1))

    def shifted_logits(k0, hh):
        return _dot_nt(kext[pl.ds(k0, ts), hh * LANES:(hh + 1) * LANES], qe[hh])

    @pl.when(worst_shift > SHIFT_CAP)
    def _():
        m_sc[...] = jnp.full(m_sc.shape, NEG_BIG, _f32)

        def max_body(kt, carry):
            k0 = pl.multiple_of(kt * ts, ts)
            bias = jnp.where(skey[pl.ds(k0, ts), :] >= thr, 0.0, -jnp.inf)
            for hh in range(A_HEADS):
                lg = shifted_logits(k0, hh) + bias
                m_sc[hh:hh + 1, :] = jnp.maximum(m_sc[hh:hh + 1, :], lg.max(axis=0, keepdims=True))
            return carry

        lax.fori_loop(0, n_kt, max_body, 0)
        eye = jnp.where(lax.broadcasted_iota(_i32, (tq, tq), 0) == lax.broadcasted_iota(_i32, (tq, tq), 1),
                        1.0, 0.0).astype(_bf16)
        m_col = _dot_nt(eye, m_sc[...].astype(_bf16))
        for hh in range(A_HEADS):
            cur = qe[hh].astype(_f32)
            qe[hh] = jnp.where(qlane == A_HEAD_DIM, cur - m_col[:, hh:hh + 1], cur).astype(_bf16)

    acc_sc[...] = jnp.zeros(acc_sc.shape, _f32)

    def att_body(kt, carry):
        k0 = pl.multiple_of(kt * ts, ts)
        bias_sc[...] = jnp.where(skey[pl.ds(k0, ts), :] >= thr, 0.0, -jnp.inf)

        def masked_logits(hh):
            lg_sc[hh % 2] = shifted_logits(k0, hh) + bias_sc[...]

        masked_logits(0)
        for hh in range(A_HEADS):
            if hh + 1 < A_HEADS:
                masked_logits(hh + 1)
            pr = jnp.exp2(lg_sc[hh % 2]).astype(_bf16)
            rows = slice(hh * V_ROWS, (hh + 1) * V_ROWS)
            acc_sc[rows, :] += _dot(vat_ref[rows, pl.ds(k0, ts)], pr)
        return carry

    lax.fori_loop(0, n_kt, att_body, 0)

    outs = []
    for hh in range(A_HEADS):
        r0 = hh * V_ROWS
        outs.append(acc_sc[r0:r0 + A_HEAD_DIM, :] / acc_sc[r0 + A_HEAD_DIM:r0 + A_HEAD_DIM + 1, :])
    o_ref[...] = jnp.concatenate(outs, axis=0).T


def _dsa_call(qi, qa, smallt, ki2, ka, vat, *, batch, seq, k_sel):
    tq = min(DSA_TQ, seq)
    ts = min(DSA_TS, seq)
    nq = seq // tq
    qtok = lambda b, q: (b * nq + q, 0)
    return pl.pallas_call(
        functools.partial(_dsa_kernel, k_sel=k_sel, ts=ts),
        grid=(batch, nq),
        in_specs=[
            pl.BlockSpec((tq, IDX_WIDTH), qtok),
            pl.BlockSpec((tq, A_WIDTH), qtok),
            pl.BlockSpec((SMALL_ROWS, tq), lambda b, q: (0, b * nq + q)),
            pl.BlockSpec((seq, LANES), lambda b, q: (b, 0)),
            pl.BlockSpec((seq, A_WIDTH), lambda b, q: (b, 0)),
            pl.BlockSpec((A_HEADS * V_ROWS, seq), lambda b, q: (0, b)),
        ],
        out_specs=pl.BlockSpec((tq, A_WIDTH), qtok),
        out_shape=jax.ShapeDtypeStruct((batch * seq, A_WIDTH), _f32),
        scratch_shapes=[
            pltpu.VMEM((seq, tq), _i32),
            pltpu.VMEM((seq, tq), _i16),
            pltpu.VMEM((IDX_HEADS, tq, LANES), _bf16),
            pltpu.VMEM((A_HEADS, tq, LANES), _bf16),
            pltpu.VMEM((seq, A_HEADS * LANES), _bf16),
            pltpu.VMEM((A_HEADS, LANES), _f32),
            pltpu.VMEM((ts, tq), _f32),
            pltpu.VMEM((2, ts, tq), _f32),
            pltpu.VMEM((A_HEADS, tq), _f32),
            pltpu.VMEM((A_HEADS * V_ROWS, tq), _f32),
        ],
        compiler_params=_params(2), name="dsa",
    )(qi, qa, smallt, ki2, ka, vat)


def _mlstm_kernel(q_ref, k_ref, v_ref, small_ref, smallt_ref, o_ref, c_sc, m_sc):
    rows = q_ref.shape[0]
    L = min(MLSTM_L, rows)
    d = B_HEAD_DIM

    @pl.when(pl.program_id(1) == 0)
    def _():
        c_sc[...] = jnp.zeros(c_sc.shape, _f32)
        m_sc[...] = jnp.zeros(m_sc.shape, _f32)

    tril = lax.broadcasted_iota(_i32, (L, L), 1) <= lax.broadcasted_iota(_i32, (L, L), 0)
    trilb = jnp.where(tril, 1.0, 0.0).astype(_bf16)
    lane0 = lax.broadcasted_iota(_i32, (L, d), 1) == 0

    def chunk_body(c, carry):
        r0 = pl.multiple_of(c * L, L)
        sm = small_ref[pl.ds(r0, L), :]
        smt = smallt_ref[:, pl.ds(r0, L)]
        bcol = sum(_dot(trilb, part) for part in _split3(sm))
        brow = sum(_dot_nt(part, trilb) for part in _split3(smt))
        for hh in range(B_HEADS):
            cols = slice(hh * d, (hh + 1) * d)
            li_c, b_c = sm[:, _S_LI + hh:_S_LI + hh + 1], bcol[:, _S_LF + hh:_S_LF + hh + 1]
            li_r, b_r = smt[_S_LI + hh:_S_LI + hh + 1, :], brow[_S_LF + hh:_S_LF + hh + 1, :]
            m_prev = m_sc[hh:hh + 1, 0:1]
            dmat = jnp.where(tril, b_c - b_r + li_r, -jnp.inf)
            m_inter = b_c + m_prev
            m_t = jnp.maximum(m_inter, dmat.max(axis=1, keepdims=True))
            w_inter = jnp.exp(m_inter - m_t)
            qh = q_ref[pl.ds(r0, L), cols]
            kh = k_ref[pl.ds(r0, L), cols]
            vh = v_ref[pl.ds(r0, L), cols]
            s_qk = _dot_nt(qh, kh) * jnp.exp(dmat - m_t)
            cext = c_sc[hh]
            inter = _dot(qh, cext.astype(_bf16))
            num = w_inter * inter[:, :d] + _dot(s_qk.astype(_bf16), vh)
            den = w_inter * inter[:, d:d + 1] + s_qk.sum(axis=1, keepdims=True)
            o_ref[pl.ds(r0, L), cols] = num / jnp.maximum(jnp.abs(den), jnp.exp(-m_t))
            b_last = b_c[L - 1:L, :]
            g_c = b_last - b_c + li_c
            m_new = jnp.maximum(b_last + m_prev, g_c.max(axis=0, keepdims=True))
            decay = jnp.exp(b_last + m_prev - m_new)
            wk = jnp.exp(g_c - m_new)
            wkv = jnp.concatenate([wk * vh.astype(_f32), jnp.where(lane0, wk, 0.0)], axis=1)
            c_sc[hh] = decay * cext + _dot_tn(kh, wkv.astype(_bf16))
            m_sc[hh:hh + 1, :] = jnp.broadcast_to(m_new, (1, LANES))
        return carry

    lax.fori_loop(0, rows // L, chunk_body, 0)


def _mlstm_call(qb, kb, vb, small, smallt, *, batch, seq):
    rows = min(MLSTM_ROWS, seq)
    nb = seq // rows
    tok = lambda b, j: (b * nb + j, 0)
    return pl.pallas_call(
        _mlstm_kernel,
        grid=(batch, nb),
        in_specs=[
            pl.BlockSpec((rows, B_WIDTH), tok),
            pl.BlockSpec((rows, B_WIDTH), tok),
            pl.BlockSpec((rows, B_WIDTH), tok),
            pl.BlockSpec((rows, LANES), tok),
            pl.BlockSpec((SMALL_ROWS, rows), lambda b, j: (0, b * nb + j)),
        ],
        out_specs=pl.BlockSpec((rows, B_WIDTH), tok),
        out_shape=jax.ShapeDtypeStruct((batch * seq, B_WIDTH), _f32),
        scratch_shapes=[
            pltpu.VMEM((B_HEADS, B_HEAD_DIM, 2 * B_HEAD_DIM), _f32),
            pltpu.VMEM((SUBLANES, LANES), _f32),
        ],
        compiler_params=_params(2), name="mlstm",
    )(qb, kb, vb, small, smallt)


_G_AZ, _G_BZ, _G_BO = 0, A_WIDTH, A_WIDTH + B_WIDTH
_G_GA = _G_BO + B_WIDTH
_G_GB = _G_GA + D_MODEL
_G_END = _G_GB + D_MODEL


def _out_kernel(x_ref, oa_ref, cell_ref, g_ref, wg_ref, wba_ref, wbb_ref, wo_ref, hng_ref, fg_ref,
                out_ref, *, final_norm):
    x = x_ref[...]
    h = _rmsnorm(x, g_ref[...]).astype(_bf16)
    a_z = _dot(h, wg_ref[:, _G_AZ:_G_AZ + A_WIDTH])
    y_a = _dot((oa_ref[...] * _silu(a_z)).astype(_bf16), wba_ref[...])

    b_o = _dot(h, wg_ref[:, _G_BO:_G_BO + B_WIDTH])
    t = jax.nn.sigmoid(b_o) * cell_ref[...]
    normed = []
    for hh in range(B_HEADS):
        th = t[:, hh * B_HEAD_DIM:(hh + 1) * B_HEAD_DIM]
        mu = jnp.mean(th, axis=-1, keepdims=True)
        var = jnp.mean(jnp.square(th - mu), axis=-1, keepdims=True)
        normed.append((th - mu) * lax.rsqrt(var + EPS))
    b_z = _dot(h, wg_ref[:, _G_BZ:_G_BZ + B_WIDTH])
    h_b = jnp.concatenate(normed, axis=1) * hng_ref[...] * _silu(b_z)
    y_b = _dot(h_b.astype(_bf16), wbb_ref[...])

    g_a = _dot(h, wg_ref[:, _G_GA:_G_GA + D_MODEL])
    g_b = _dot(h, wg_ref[:, _G_GB:_G_GB + D_MODEL])
    merged = jax.nn.sigmoid(g_a) * y_a + jax.nn.sigmoid(g_b) * y_b
    y = x + _dot(merged.astype(_bf16), wo_ref[...])
    if final_norm:
        y = _rmsnorm(y, fg_ref[...])
    out_ref[...] = y


def _out_call(x2, oa, cell, g, wg, wba, wbb, wo, hng, fg, *, final_norm):
    tokens = x2.shape[0]
    rows = min(OUT_ROWS, tokens)
    const = lambda i: (0, 0)
    tok = lambda i: (i, 0)
    return pl.pallas_call(
        functools.partial(_out_kernel, final_norm=final_norm),
        grid=(tokens // rows,),
        in_specs=[
            pl.BlockSpec((rows, D_MODEL), tok),
            pl.BlockSpec((rows, A_WIDTH), tok),
            pl.BlockSpec((rows, B_WIDTH), tok),
            pl.BlockSpec((1, D_MODEL), const),
            pl.BlockSpec((D_MODEL, _G_END), const),
            pl.BlockSpec((A_WIDTH, D_MODEL), const),
            pl.BlockSpec((B_WIDTH, D_MODEL), const),
            pl.BlockSpec((D_MODEL, D_MODEL), const),
            pl.BlockSpec((1, B_WIDTH), const),
            pl.BlockSpec((1, D_MODEL), const),
        ],
        out_specs=pl.BlockSpec((rows, D_MODEL), tok),
        out_shape=jax.ShapeDtypeStruct((tokens, D_MODEL), _f32),
        compiler_params=_params(1), name="out",
    )(x2, oa, cell, g, wg, wba, wbb, wo, hng, fg)


def _rope_tables(seq):
    inv = ROPE_THETA ** (-(jnp.arange(ROPE_HALF, dtype=_f32) * 2.0 / (2 * ROPE_HALF)))
    ang = jnp.arange(seq).astype(_f32)[:, None] * inv[None, :]
    cos, sin = jnp.cos(ang), jnp.sin(ang)
    pad = jnp.zeros((seq, A_HEAD_DIM - 2 * ROPE_HALF), _f32)
    zero = jnp.zeros((seq, ROPE_HALF), _f32)
    cos_h = jnp.concatenate([cos, cos, pad + 1.0], axis=1)
    sa_h = jnp.concatenate([-sin, zero, pad], axis=1)
    sb_h = jnp.concatenate([zero, sin, pad], axis=1)
    rep = LANES // A_HEAD_DIM
    return tuple(jnp.tile(t, (1, rep)) for t in (cos_h, sa_h, sb_h))


def kernel(x, norm_g, w_in, conv_w, conv_b, b_igate, b_fgate, head_norm_g,
           w_branch_a, w_branch_b, w_out, final_norm_g):
    batch, seq, _ = x.shape
    depth = w_in.shape[0]
    k_sel = min(TOPK_MAX, seq // 4)
    offsets = np.cumsum(IN_SIZES)[:-1]
    cos_t, sa_t, sb_t = _rope_tables(seq)
    x2 = x.reshape(batch * seq, D_MODEL)
    for l in range(depth):
        (a_q, a_k, a_v, a_z, i_q, i_k, i_w, b_q, b_k, b_v, b_z, b_o, b_i, b_f, g_a, g_b) = \
            jnp.split(w_in[l], offsets, axis=-1)
        small_w = jnp.concatenate(
            [b_i, b_f, i_w, jnp.zeros((D_MODEL, LANES - 2 * B_HEADS - IDX_HEADS), _f32)], axis=1)
        w_proj = jnp.concatenate([a_q, a_k, i_q, b_q, b_k, b_v, i_k, i_k, small_w], axis=1).astype(_bf16)
        wvt = a_v.T.astype(_bf16)
        wst = small_w[:, :SMALL_ROWS].T.astype(_bf16)
        bias = jnp.concatenate(
            [b_igate[l], b_fgate[l], jnp.zeros((LANES - 2 * B_HEADS,), _f32)])
        biast = jnp.broadcast_to(bias[:SMALL_ROWS, None], (SMALL_ROWS, LANES))
        w_gate = jnp.concatenate([a_z, b_z, b_o, g_a, g_b], axis=1).astype(_bf16)
        g = norm_g[l].reshape(1, D_MODEL)

        qa, ka, vat, qi, ki2, qb, kb, vb, small, smallt = _proj_call(
            x2, g, w_proj, wvt, wst, cos_t, sa_t, sb_t, conv_w[l], conv_b[l].reshape(1, -1),
            bias.reshape(1, LANES), biast, seq=seq)
        o_a = _dsa_call(qi, qa, smallt, ki2, ka, vat, batch=batch, seq=seq, k_sel=k_sel)
        cell = _mlstm_call(qb, kb, vb, small, smallt, batch=batch, seq=seq)
        x2 = _out_call(x2, o_a, cell, g, w_gate, w_branch_a[l].astype(_bf16),
                       w_branch_b[l].astype(_bf16), w_out[l].astype(_bf16),
                       head_norm_g[l].reshape(1, B_WIDTH), final_norm_g.reshape(1, D_MODEL),
                       final_norm=(l == depth - 1))
    return x2.reshape(batch, seq, D_MODEL)
```

```python
import functools

import jax
import jax.numpy as jnp
import numpy as np
from jax import lax
from jax.experimental import pallas as pl
from jax.experimental.pallas import tpu as pltpu

D_MODEL = 1024
CHUNK = 64
ROPE_THETA = 500000.0
EPS = 1e-6
A_HEADS, A_HEAD_DIM = 8, 64
A_WIDTH = A_HEADS * A_HEAD_DIM
IDX_HEADS, IDX_DIM = 16, 64
IDX_WIDTH = IDX_HEADS * IDX_DIM
TOPK_MAX = 256
B_HEADS, B_HEAD_DIM = 4, 128
B_WIDTH = B_HEADS * B_HEAD_DIM
CONV_WIDTH = 4
IN_SIZES = (A_WIDTH,) * 4 + (IDX_WIDTH, IDX_DIM, IDX_HEADS) + (B_WIDTH,) * 5 \
    + (B_HEADS, B_HEADS, D_MODEL, D_MODEL)

LANES = 128
SUBLANES = 8
VMEM_LIMIT_BYTES = 56 * 1024 * 1024

PROJ_ROWS = 512
CONV_HALO = SUBLANES
DSA_TQ = 256
DSA_TS = 512
MLSTM_L = 128
MLSTM_ROWS = 1024
OUT_ROWS = 256
ROPE_HALF = A_HEAD_DIM // 8
SMALL_ROWS = 32
INT_MIN = -2 ** 31
HALF_BITS = 16
HALF_MIN = -2 ** (HALF_BITS - 1)
HALF_MASK = 2 ** HALF_BITS - 1
PACKED_SUBLANES = 2 * SUBLANES
COUNT_CHAINS = 4
LOG2E = 1.4426950408889634
NEG_BIG = -1e30
V_ROWS = A_HEAD_DIM + PACKED_SUBLANES
SHIFT_SLACK = 1.02
SHIFT_CAP = 40.0

assert CONV_HALO >= CONV_WIDTH - 1 and IDX_DIM == A_HEAD_DIM == LANES // 2 and B_HEAD_DIM == LANES

_f32 = jnp.float32
_bf16 = jnp.bfloat16
_i32 = jnp.int32
_i16 = jnp.int16


def _dot(a, b):
    return jnp.dot(a, b, preferred_element_type=_f32)


def _dot_nt(a, b):
    return lax.dot_general(a, b, (((1,), (1,)), ((), ())), preferred_element_type=_f32)


def _dot_tn(a, b):
    return lax.dot_general(a, b, (((0,), (0,)), ((), ())), preferred_element_type=_f32)


def _split3(x):
    hi = x.astype(_bf16)
    r1 = x - hi.astype(_f32)
    mid = r1.astype(_bf16)
    lo = (r1 - mid.astype(_f32)).astype(_bf16)
    return hi, mid, lo


def _rmsnorm(x, g):
    ms = jnp.mean(x * x, axis=-1, keepdims=True)
    return x * lax.rsqrt(ms + EPS) * g


def _silu(x):
    return x * jax.nn.sigmoid(x)


def _params(n_axes):
    return pltpu.CompilerParams(dimension_semantics=("arbitrary",) * n_axes,
                                vmem_limit_bytes=VMEM_LIMIT_BYTES)


_C_AQ = 0
_C_AK = _C_AQ + A_WIDTH
_C_IQ = _C_AK + A_WIDTH
_C_BQK = _C_IQ + IDX_WIDTH
_C_BV = _C_BQK + 2 * B_WIDTH
_C_IK2 = _C_BV + B_WIDTH
_C_SMALL = _C_IK2 + LANES
_C_END = _C_SMALL + LANES
_S_LI, _S_LF, _S_W = 0, B_HEADS, 2 * B_HEADS


def _rope(x, cos_t, sa_t, sb_t):
    outs = []
    for g in range(x.shape[1] // LANES):
        xs = x[:, g * LANES:(g + 1) * LANES]
        up = pltpu.roll(xs, LANES - ROPE_HALF, 1)
        dn = pltpu.roll(xs, ROPE_HALF, 1)
        outs.append(xs * cos_t + up * sa_t + dn * sb_t)
    return outs[0] if len(outs) == 1 else jnp.concatenate(outs, axis=1)


def _finish_small(t, idx):
    lf = jax.nn.log_sigmoid(t)
    return jnp.where((idx >= _S_LF) & (idx < _S_W), lf,
                     jnp.where(idx >= _S_W, t * (IDX_HEADS ** -0.5), t))


def _proj_kernel(x_ref, g_ref, w_ref, wvt_ref, wst_ref, cos_ref, sa_ref, sb_ref, cw_ref, cb_ref,
                 bias_ref, biast_ref,
                 qa_ref, ka_ref, vat_ref, qi_ref, ki2_ref, qb_ref, kb_ref, vb_ref,
                 small_ref, smallt_ref, ubuf, *, tiles_per_seq):
    i = pl.program_id(0)
    rows = x_ref.shape[0]
    h = _rmsnorm(x_ref[...], g_ref[...]).astype(_bf16)
    cos_t, sa_t, sb_t = cos_ref[...], sa_ref[...], sb_ref[...]

    qa = _rope(_dot(h, w_ref[:, _C_AQ:_C_AQ + A_WIDTH]), cos_t, sa_t, sb_t)
    qa_ref[...] = (qa * (A_HEAD_DIM ** -0.5 * LOG2E)).astype(_bf16)
    ka_ref[...] = _rope(_dot(h, w_ref[:, _C_AK:_C_AK + A_WIDTH]), cos_t, sa_t, sb_t).astype(_bf16)
    vat = _dot_nt(wvt_ref[...], h).reshape(A_HEADS, A_HEAD_DIM, rows)
    tail = jnp.where(lax.broadcasted_iota(_i32, (A_HEADS, PACKED_SUBLANES, rows), 1) == 0, 1.0, 0.0)
    vat_ref[...] = jnp.concatenate([vat, tail], axis=1).reshape(A_HEADS * V_ROWS, rows).astype(_bf16)
    qi = _rope(_dot(h, w_ref[:, _C_IQ:_C_IQ + IDX_WIDTH]), cos_t, sa_t, sb_t)
    qi_ref[...] = (qi * (IDX_DIM ** -0.5)).astype(_bf16)
    ki2_ref[...] = _rope(_dot(h, w_ref[:, _C_IK2:_C_IK2 + LANES]), cos_t, sa_t, sb_t).astype(_bf16)

    @pl.when(i % tiles_per_seq == 0)
    def _():
        ubuf[0:CONV_HALO, :] = jnp.zeros((CONV_HALO, 2 * B_WIDTH), _f32)

    ubuf[CONV_HALO:CONV_HALO + rows, :] = _dot(h, w_ref[:, _C_BQK:_C_BQK + 2 * B_WIDTH])
    conv = None
    for j in range(CONV_WIDTH):
        off = CONV_HALO - (CONV_WIDTH - 1) + j
        term = ubuf[off:off + rows, :] * cw_ref[j:j + 1, :]
        conv = term if conv is None else conv + term
    conv = conv + cb_ref[...]
    ubuf[0:CONV_HALO, :] = ubuf[rows:rows + CONV_HALO, :]
    qk = _silu(conv)
    qb_ref[...] = qk[:, :B_WIDTH].astype(_bf16)
    kb_ref[...] = (qk[:, B_WIDTH:] * (B_HEAD_DIM ** -0.5)).astype(_bf16)
    vb_ref[...] = _dot(h, w_ref[:, _C_BV:_C_BV + B_WIDTH]).astype(_bf16)

    sm = _dot(h, w_ref[:, _C_SMALL:_C_SMALL + LANES]) + bias_ref[...]
    small_ref[...] = _finish_small(sm, lax.broadcasted_iota(_i32, sm.shape, 1))
    smt = _dot_nt(wst_ref[...], h) + biast_ref[:, 0:1]
    smallt_ref[...] = _finish_small(smt, lax.broadcasted_iota(_i32, smt.shape, 0))


def _proj_call(x2, g, w, wvt, wst, cos_t, sa_t, sb_t, cw, cb, bias, biast, *, seq):
    tokens = x2.shape[0]
    rows = min(PROJ_ROWS, seq)
    tiles_per_seq = seq // rows
    grid = (tokens // rows,)
    const = lambda i: (0, 0)
    tok = lambda i: (i, 0)
    tok_t = lambda i: (0, i)
    pos = lambda i: (i % tiles_per_seq, 0)
    in_specs = [
        pl.BlockSpec((rows, D_MODEL), tok),
        pl.BlockSpec((1, D_MODEL), const),
        pl.BlockSpec((D_MODEL, _C_END), const),
        pl.BlockSpec((A_WIDTH, D_MODEL), const),
        pl.BlockSpec((SMALL_ROWS, D_MODEL), const),
        pl.BlockSpec((rows, LANES), pos),
        pl.BlockSpec((rows, LANES), pos),
        pl.BlockSpec((rows, LANES), pos),
        pl.BlockSpec((CONV_WIDTH, 2 * B_WIDTH), const),
        pl.BlockSpec((1, 2 * B_WIDTH), const),
        pl.BlockSpec((1, LANES), const),
        pl.BlockSpec((SMALL_ROWS, LANES), const),
    ]
    out_shape = [
        jax.ShapeDtypeStruct((tokens, A_WIDTH), _bf16),
        jax.ShapeDtypeStruct((tokens, A_WIDTH), _bf16),
        jax.ShapeDtypeStruct((A_HEADS * V_ROWS, tokens), _bf16),
        jax.ShapeDtypeStruct((tokens, IDX_WIDTH), _bf16),
        jax.ShapeDtypeStruct((tokens, LANES), _bf16),
        jax.ShapeDtypeStruct((tokens, B_WIDTH), _bf16),
        jax.ShapeDtypeStruct((tokens, B_WIDTH), _bf16),
        jax.ShapeDtypeStruct((tokens, B_WIDTH), _bf16),
        jax.ShapeDtypeStruct((tokens, LANES), _f32),
        jax.ShapeDtypeStruct((SMALL_ROWS, tokens), _f32),
    ]
    out_specs = [
        pl.BlockSpec((rows, A_WIDTH), tok),
        pl.BlockSpec((rows, A_WIDTH), tok),
        pl.BlockSpec((A_HEADS * V_ROWS, rows), tok_t),
        pl.BlockSpec((rows, IDX_WIDTH), tok),
        pl.BlockSpec((rows, LANES), tok),
        pl.BlockSpec((rows, B_WIDTH), tok),
        pl.BlockSpec((rows, B_WIDTH), tok),
        pl.BlockSpec((rows, B_WIDTH), tok),
        pl.BlockSpec((rows, LANES), tok),
        pl.BlockSpec((SMALL_ROWS, rows), tok_t),
    ]
    return pl.pallas_call(
        functools.partial(_proj_kernel, tiles_per_seq=tiles_per_seq),
        grid=grid, in_specs=in_specs, out_specs=out_specs, out_shape=out_shape,
        scratch_shapes=[pltpu.VMEM((rows + CONV_HALO, 2 * B_WIDTH), _f32)],
        compiler_params=_params(1), name="proj",
    )(x2, g, w, wvt, wst, cos_t, sa_t, sb_t, cw, cb, bias, biast)


def _sortable(x):
    bits = pltpu.bitcast(x, _i32)
    return jnp.where(bits >= 0, bits, bits ^ jnp.int32(0x7FFFFFFF))


def _head_halves(pair_f32, lane):
    lo = lane < A_HEAD_DIM
    return (jnp.where(lo, pair_f32, 0.0),
            jnp.where(lo, pltpu.roll(pair_f32, A_HEAD_DIM, 1), 0.0))


def _dsa_kernel(qi_ref, qa_ref, wt_ref, ki2_ref, ka_ref, vat_ref, o_ref,
                skey, half, qim, qe, kext, kmax_sc, bias_sc, lg_sc, m_sc, acc_sc, *, k_sel, ts):
    qt = pl.program_id(1)
    tq = qi_ref.shape[0]
    seq = ka_ref.shape[0]
    q0 = qt * tq
    n_kt = (q0 + tq + ts - 1) // ts

    @pl.when(qt == 0)
    def _():
        klane = lax.broadcasted_iota(_i32, (ts, LANES), 1)
        one_hot = jnp.where(klane == A_HEAD_DIM, 1.0, 0.0)
        kmax_sc[...] = jnp.zeros(kmax_sc.shape, _f32)

        def kext_body(kt, carry):
            k0 = pl.multiple_of(kt * ts, ts)
            for p in range(A_HEADS // 2):
                pair = ka_ref[pl.ds(k0, ts), p * LANES:(p + 1) * LANES].astype(_f32)
                for j, kh in enumerate(_head_halves(pair, klane)):
                    hh = 2 * p + j
                    kext[pl.ds(k0, ts), hh * LANES:(hh + 1) * LANES] = (kh + one_hot).astype(_bf16)
                    n2 = jnp.sum(kh * kh, axis=1, keepdims=True).max(axis=0, keepdims=True)
                    kmax_sc[hh:hh + 1, :] = jnp.maximum(kmax_sc[hh:hh + 1, :], n2)
            return carry

        lax.fori_loop(0, seq // ts, kext_body, 0)

    qlane = lax.broadcasted_iota(_i32, (tq, LANES), 1)
    lo_half = qlane < (LANES // 2)
    zero = jnp.zeros((tq, LANES), _bf16)
    for p in range(IDX_HEADS // 2):
        pair = qi_ref[:, p * LANES:(p + 1) * LANES]
        qim[2 * p] = jnp.where(lo_half, pair, zero)
        qim[2 * p + 1] = jnp.where(lo_half, zero, pair)
    ones_sq = jnp.ones((LANES, LANES), _bf16)
    worst = jnp.zeros((tq, LANES), _f32)
    for p in range(A_HEADS // 2):
        pair = qa_ref[:, p * LANES:(p + 1) * LANES].astype(_f32)
        for j, qh in enumerate(_head_halves(pair, qlane)):
            hh = 2 * p + j
            q2 = _dot((qh * qh).astype(_bf16), ones_sq)
            prod = jnp.maximum(q2 * kmax_sc[hh:hh + 1, :], 1e-30)
            shift = prod * lax.rsqrt(prod) * SHIFT_SLACK
            qe[hh] = jnp.where(qlane == A_HEAD_DIM, -shift, qh).astype(_bf16)
            worst = jnp.maximum(worst, shift)
    worst_shift = jnp.max(worst)

    qchunk = (q0 + lax.broadcasted_iota(_i32, (ts, tq), 1)) // CHUNK
    krow = lax.broadcasted_iota(_i32, (ts, tq), 0)

    def score_body(kt, carry):
        k0 = pl.multiple_of(kt * ts, ts)
        kslab = ki2_ref[pl.ds(k0, ts), :]
        acc = jnp.zeros((ts, tq), _f32)
        for hh in range(IDX_HEADS):
            s = _dot_nt(kslab, qim[hh])
            acc = acc + jnp.maximum(s, 0.0) * wt_ref[_S_W + hh:_S_W + hh + 1, :]
        admissible = (k0 + krow) // CHUNK <= qchunk
        key = jnp.where(admissible, _sortable(acc), jnp.int32(INT_MIN))
        skey[pl.ds(k0, ts), :] = key
        half[pl.ds(k0, ts), :] = (key >> HALF_BITS).astype(_i16)
        return carry

    lax.fori_loop(0, n_kt, score_body, 0)

    def search16(need):
        def bit_body(i, carry):
            t, above = carry
            cand = t + jnp.left_shift(jnp.int32(1), HALF_BITS - 1 - i)
            cand16 = cand.astype(_i16)

            def count_body(kt, cnt):
                k0 = pl.multiple_of(kt * ts, ts)
                tile = half[pl.ds(k0, ts), :]
                sums = [None] * COUNT_CHAINS
                for n, r in enumerate(range(0, ts, PACKED_SUBLANES)):
                    ge = jnp.where(tile[r:r + PACKED_SUBLANES] >= cand16, _bf16(1), _bf16(0))
                    c = n % COUNT_CHAINS
                    sums[c] = ge if sums[c] is None else sums[c] + ge
                while len(sums) > 1:
                    sums = [a + b for a, b in zip(sums[0::2], sums[1::2])]
                return cnt + sums[0].astype(_f32)

            cnt = lax.fori_loop(0, n_kt, count_body, jnp.zeros((PACKED_SUBLANES, tq), _f32))
            total = cnt.sum(axis=0, keepdims=True)
            ok = total >= need
            return jnp.where(ok, cand, t), jnp.where(ok, above, total)

        return lax.fori_loop(0, HALF_BITS, bit_body,
                             (jnp.full((1, tq), HALF_MIN, _i32), jnp.zeros((1, tq), _f32)))

    thr_hi, above = search16(jnp.float32(k_sel))

    def low_body(kt, carry):
        k0 = pl.multiple_of(kt * ts, ts)
        key = skey[pl.ds(k0, ts), :]
        low = (key & jnp.int32(HALF_MASK)) + jnp.int32(HALF_MIN)
        half[pl.ds(k0, ts), :] = jnp.where((key >> HALF_BITS) == thr_hi, low,
                                           jnp.int32(HALF_MIN)).astype(_i16)
        return carry

    lax.fori_loop(0, n_kt, low_body, 0)
    thr_lo, _ = search16(jnp.float32(k_sel) - above)
    thr = jnp.left_shift(thr_hi, HALF_BITS) + (thr_lo - jnp.int32(HALF_MIN))
    thr = jnp.maximum(thr, jnp.int32(INT_MIN + 1))

    def shifted_logits(k0, hh):
        return _dot_nt(kext[pl.ds(k0, ts), hh * LANES:(hh + 1) * LANES], qe[hh])

    @pl.when(worst_shift > SHIFT_CAP)
    def _():
        m_sc[...] = jnp.full(m_sc.shape, NEG_BIG, _f32)

        def max_body(kt, carry):
            k0 = pl.multiple_of(kt * ts, ts)
            bias = jnp.where(skey[pl.ds(k0, ts), :] >= thr, 0.0, -jnp.inf)
            for hh in range(A_HEADS):
                lg = shifted_logits(k0, hh) + bias
                m_sc[hh:hh + 1, :] = jnp.maximum(m_sc[hh:hh + 1, :], lg.max(axis=0, keepdims=True))
            return carry

        lax.fori_loop(0, n_kt, max_body, 0)
        eye = jnp.where(lax.broadcasted_iota(_i32, (tq, tq), 0) == lax.broadcasted_iota(_i32, (tq, tq), 1),
                        1.0, 0.0).astype(_bf16)
        m_col = _dot_nt(eye, m_sc[...].astype(_bf16))
        for hh in range(A_HEADS):
            cur = qe[hh].astype(_f32)
            qe[hh] = jnp.where(qlane == A_HEAD_DIM, cur - m_col[:, hh:hh + 1], cur).astype(_bf16)

    acc_sc[...] = jnp.zeros(acc_sc.shape, _f32)

    def att_body(kt, carry):
        k0 = pl.multiple_of(kt * ts, ts)
        bias_sc[...] = jnp.where(skey[pl.ds(k0, ts), :] >= thr, 0.0, -jnp.inf)

        def masked_logits(hh):
            lg_sc[hh % 2] = shifted_logits(k0, hh) + bias_sc[...]

        masked_logits(0)
        for hh in range(A_HEADS):
            if hh + 1 < A_HEADS:
                masked_logits(hh + 1)
            pr = jnp.exp2(lg_sc[hh % 2]).astype(_bf16)
            rows = slice(hh * V_ROWS, (hh + 1) * V_ROWS)
            acc_sc[rows, :] += _dot(vat_ref[rows, pl.ds(k0, ts)], pr)
        return carry

    lax.fori_loop(0, n_kt, att_body, 0)

    outs = []
    for hh in range(A_HEADS):
        r0 = hh * V_ROWS
        outs.append(acc_sc[r0:r0 + A_HEAD_DIM, :] / acc_sc[r0 + A_HEAD_DIM:r0 + A_HEAD_DIM + 1, :])
    o_ref[...] = jnp.concatenate(outs, axis=0).T


def _dsa_call(qi, qa, smallt, ki2, ka, vat, *, batch, seq, k_sel):
    tq = min(DSA_TQ, seq)
    ts = min(DSA_TS, seq)
    nq = seq // tq
    qtok = lambda b, q: (b * nq + q, 0)
    return pl.pallas_call(
        functools.partial(_dsa_kernel, k_sel=k_sel, ts=ts),
        grid=(batch, nq),
        in_specs=[
            pl.BlockSpec((tq, IDX_WIDTH), qtok),
            pl.BlockSpec((tq, A_WIDTH), qtok),
            pl.BlockSpec((SMALL_ROWS, tq), lambda b, q: (0, b * nq + q)),
            pl.BlockSpec((seq, LANES), lambda b, q: (b, 0)),
            pl.BlockSpec((seq, A_WIDTH), lambda b, q: (b, 0)),
            pl.BlockSpec((A_HEADS * V_ROWS, seq), lambda b, q: (0, b)),
        ],
        out_specs=pl.BlockSpec((tq, A_WIDTH), qtok),
        out_shape=jax.ShapeDtypeStruct((batch * seq, A_WIDTH), _f32),
        scratch_shapes=[
            pltpu.VMEM((seq, tq), _i32),
            pltpu.VMEM((seq, tq), _i16),
            pltpu.VMEM((IDX_HEADS, tq, LANES), _bf16),
            pltpu.VMEM((A_HEADS, tq, LANES), _bf16),
            pltpu.VMEM((seq, A_HEADS * LANES), _bf16),
            pltpu.VMEM((A_HEADS, LANES), _f32),
            pltpu.VMEM((ts, tq), _f32),
            pltpu.VMEM((2, ts, tq), _f32),
            pltpu.VMEM((A_HEADS, tq), _f32),
            pltpu.VMEM((A_HEADS * V_ROWS, tq), _f32),
        ],
        compiler_params=_params(2), name="dsa",
    )(qi, qa, smallt, ki2, ka, vat)


def _mlstm_kernel(q_ref, k_ref, v_ref, small_ref, smallt_ref, o_ref, c_sc, m_sc):
    rows = q_ref.shape[0]
    L = min(MLSTM_L, rows)
    d = B_HEAD_DIM

    @pl.when(pl.program_id(1) == 0)
    def _():
        c_sc[...] = jnp.zeros(c_sc.shape, _f32)
        m_sc[...] = jnp.zeros(m_sc.shape, _f32)

    tril = lax.broadcasted_iota(_i32, (L, L), 1) <= lax.broadcasted_iota(_i32, (L, L), 0)
    trilb = jnp.where(tril, 1.0, 0.0).astype(_bf16)
    lane0 = lax.broadcasted_iota(_i32, (L, d), 1) == 0

    def chunk_body(c, carry):
        r0 = pl.multiple_of(c * L, L)
        sm = small_ref[pl.ds(r0, L), :]
        smt = smallt_ref[:, pl.ds(r0, L)]
        bcol = sum(_dot(trilb, part) for part in _split3(sm))
        brow = sum(_dot_nt(part, trilb) for part in _split3(smt))
        loaded = []
        for hh in range(B_HEADS):
            cols = slice(hh * d, (hh + 1) * d)
            loaded.append((q_ref[pl.ds(r0, L), cols], k_ref[pl.ds(r0, L), cols], v_ref[pl.ds(r0, L), cols],
                           c_sc[hh], m_sc[hh:hh + 1, 0:1]))
        results = []
        for hh in range(B_HEADS):
            qh, kh, vh, cext, m_prev = loaded[hh]
            li_c, b_c = sm[:, _S_LI + hh:_S_LI + hh + 1], bcol[:, _S_LF + hh:_S_LF + hh + 1]
            li_r, b_r = smt[_S_LI + hh:_S_LI + hh + 1, :], brow[_S_LF + hh:_S_LF + hh + 1, :]
            dmat = jnp.where(tril, b_c - b_r + li_r, -jnp.inf)
            m_inter = b_c + m_prev
            m_t = jnp.maximum(m_inter, dmat.max(axis=1, keepdims=True))
            w_inter = jnp.exp(m_inter - m_t)
            s_qk = _dot_nt(qh, kh) * jnp.exp(dmat - m_t)
            inter = _dot(qh, cext.astype(_bf16))
            num = w_inter * inter[:, :d] + _dot(s_qk.astype(_bf16), vh)
            den = w_inter * inter[:, d:d + 1] + s_qk.sum(axis=1, keepdims=True)
            hout = num / jnp.maximum(jnp.abs(den), jnp.exp(-m_t))
            b_last = b_c[L - 1:L, :]
            g_c = b_last - b_c + li_c
            m_new = jnp.maximum(b_last + m_prev, g_c.max(axis=0, keepdims=True))
            decay = jnp.exp(b_last + m_prev - m_new)
            wk = jnp.exp(g_c - m_new)
            wkv = jnp.concatenate([wk * vh.astype(_f32), jnp.where(lane0, wk, 0.0)], axis=1)
            results.append((hout, decay * cext + _dot_tn(kh, wkv.astype(_bf16)), m_new))
        for hh, (hout, c_new, m_new) in enumerate(results):
            o_ref[pl.ds(r0, L), hh * d:(hh + 1) * d] = hout
            c_sc[hh] = c_new
            m_sc[hh:hh + 1, :] = jnp.broadcast_to(m_new, (1, LANES))
        return carry

    lax.fori_loop(0, rows // L, chunk_body, 0)


def _mlstm_call(qb, kb, vb, small, smallt, *, batch, seq):
    rows = min(MLSTM_ROWS, seq)
    nb = seq // rows
    tok = lambda b, j: (b * nb + j, 0)
    return pl.pallas_call(
        _mlstm_kernel,
        grid=(batch, nb),
        in_specs=[
            pl.BlockSpec((rows, B_WIDTH), tok),
            pl.BlockSpec((rows, B_WIDTH), tok),
            pl.BlockSpec((rows, B_WIDTH), tok),
            pl.BlockSpec((rows, LANES), tok),
            pl.BlockSpec((SMALL_ROWS, rows), lambda b, j: (0, b * nb + j)),
        ],
        out_specs=pl.BlockSpec((rows, B_WIDTH), tok),
        out_shape=jax.ShapeDtypeStruct((batch * seq, B_WIDTH), _f32),
        scratch_shapes=[
            pltpu.VMEM((B_HEADS, B_HEAD_DIM, 2 * B_HEAD_DIM), _f32),
            pltpu.VMEM((SUBLANES, LANES), _f32),
        ],
        compiler_params=_params(2), name="mlstm",
    )(qb, kb, vb, small, smallt)


_G_AZ, _G_BZ, _G_BO = 0, A_WIDTH, A_WIDTH + B_WIDTH
_G_GA = _G_BO + B_WIDTH
_G_GB = _G_GA + D_MODEL
_G_END = _G_GB + D_MODEL


def _out_kernel(x_ref, oa_ref, cell_ref, g_ref, wg_ref, wba_ref, wbb_ref, wo_ref, hng_ref, fg_ref,
                out_ref, *, final_norm):
    x = x_ref[...]
    h = _rmsnorm(x, g_ref[...]).astype(_bf16)
    a_z = _dot(h, wg_ref[:, _G_AZ:_G_AZ + A_WIDTH])
    y_a = _dot((oa_ref[...] * _silu(a_z)).astype(_bf16), wba_ref[...])

    b_o = _dot(h, wg_ref[:, _G_BO:_G_BO + B_WIDTH])
    t = jax.nn.sigmoid(b_o) * cell_ref[...]
    normed = []
    for hh in range(B_HEADS):
        th = t[:, hh * B_HEAD_DIM:(hh + 1) * B_HEAD_DIM]
        mu = jnp.mean(th, axis=-1, keepdims=True)
        var = jnp.mean(jnp.square(th - mu), axis=-1, keepdims=True)
        normed.append((th - mu) * lax.rsqrt(var + EPS))
    b_z = _dot(h, wg_ref[:, _G_BZ:_G_BZ + B_WIDTH])
    h_b = jnp.concatenate(normed, axis=1) * hng_ref[...] * _silu(b_z)
    y_b = _dot(h_b.astype(_bf16), wbb_ref[...])

    g_a = _dot(h, wg_ref[:, _G_GA:_G_GA + D_MODEL])
    g_b = _dot(h, wg_ref[:, _G_GB:_G_GB + D_MODEL])
    merged = jax.nn.sigmoid(g_a) * y_a + jax.nn.sigmoid(g_b) * y_b
    y = x + _dot(merged.astype(_bf16), wo_ref[...])
    if final_norm:
        y = _rmsnorm(y, fg_ref[...])
    out_ref[...] = y


def _out_call(x2, oa, cell, g, wg, wba, wbb, wo, hng, fg, *, final_norm):
    tokens = x2.shape[0]
    rows = min(OUT_ROWS, tokens)
    const = lambda i: (0, 0)
    tok = lambda i: (i, 0)
    return pl.pallas_call(
        functools.partial(_out_kernel, final_norm=final_norm),
        grid=(tokens // rows,),
        in_specs=[
            pl.BlockSpec((rows, D_MODEL), tok),
            pl.BlockSpec((rows, A_WIDTH), tok),
            pl.BlockSpec((rows, B_WIDTH), tok),
            pl.BlockSpec((1, D_MODEL), const),
            pl.BlockSpec((D_MODEL, _G_END), const),
            pl.BlockSpec((A_WIDTH, D_MODEL), const),
            pl.BlockSpec((B_WIDTH, D_MODEL), const),
            pl.BlockSpec((D_MODEL, D_MODEL), const),
            pl.BlockSpec((1, B_WIDTH), const),
            pl.BlockSpec((1, D_MODEL), const),
        ],
        out_specs=pl.BlockSpec((rows, D_MODEL), tok),
        out_shape=jax.ShapeDtypeStruct((tokens, D_MODEL), _f32),
        compiler_params=_params(1), name="out",
    )(x2, oa, cell, g, wg, wba, wbb, wo, hng, fg)


def _rope_tables(seq):
    inv = ROPE_THETA ** (-(jnp.arange(ROPE_HALF, dtype=_f32) * 2.0 / (2 * ROPE_HALF)))
    ang = jnp.arange(seq).astype(_f32)[:, None] * inv[None, :]
    cos, sin = jnp.cos(ang), jnp.sin(ang)
    pad = jnp.zeros((seq, A_HEAD_DIM - 2 * ROPE_HALF), _f32)
    zero = jnp.zeros((seq, ROPE_HALF), _f32)
    cos_h = jnp.concatenate([cos, cos, pad + 1.0], axis=1)
    sa_h = jnp.concatenate([-sin, zero, pad], axis=1)
    sb_h = jnp.concatenate([zero, sin, pad], axis=1)
    rep = LANES // A_HEAD_DIM
    return tuple(jnp.tile(t, (1, rep)) for t in (cos_h, sa_h, sb_h))


def kernel(x, norm_g, w_in, conv_w, conv_b, b_igate, b_fgate, head_norm_g,
           w_branch_a, w_branch_b, w_out, final_norm_g):
    batch, seq, _ = x.shape
    depth = w_in.shape[0]
    k_sel = min(TOPK_MAX, seq // 4)
    offsets = np.cumsum(IN_SIZES)[:-1]
    cos_t, sa_t, sb_t = _rope_tables(seq)
    x2 = x.reshape(batch * seq, D_MODEL)
    for l in range(depth):
        (a_q, a_k, a_v, a_z, i_q, i_k, i_w, b_q, b_k, b_v, b_z, b_o, b_i, b_f, g_a, g_b) = \
            jnp.split(w_in[l], offsets, axis=-1)
        small_w = jnp.concatenate(
            [b_i, b_f, i_w, jnp.zeros((D_MODEL, LANES - 2 * B_HEADS - IDX_HEADS), _f32)], axis=1)
        w_proj = jnp.concatenate([a_q, a_k, i_q, b_q, b_k, b_v, i_k, i_k, small_w], axis=1).astype(_bf16)
        wvt = a_v.T.astype(_bf16)
        wst = small_w[:, :SMALL_ROWS].T.astype(_bf16)
        bias = jnp.concatenate(
            [b_igate[l], b_fgate[l], jnp.zeros((LANES - 2 * B_HEADS,), _f32)])
        biast = jnp.broadcast_to(bias[:SMALL_ROWS, None], (SMALL_ROWS, LANES))
        w_gate = jnp.concatenate([a_z, b_z, b_o, g_a, g_b], axis=1).astype(_bf16)
        g = norm_g[l].reshape(1, D_MODEL)

        qa, ka, vat, qi, ki2, qb, kb, vb, small, smallt = _proj_call(
            x2, g, w_proj, wvt, wst, cos_t, sa_t, sb_t, conv_w[l], conv_b[l].reshape(1, -1),
            bias.reshape(1, LANES), biast, seq=seq)
        o_a = _dsa_call(qi, qa, smallt, ki2, ka, vat, batch=batch, seq=seq, k_sel=k_sel)
        cell = _mlstm_call(qb, kb, vb, small, smallt, batch=batch, seq=seq)
        x2 = _out_call(x2, o_a, cell, g, w_gate, w_branch_a[l].astype(_bf16),
                       w_branch_b[l].astype(_bf16), w_out[l].astype(_bf16),
                       head_norm_g[l].reshape(1, B_WIDTH), final_norm_g.reshape(1, D_MODEL),
                       final_norm=(l == depth - 1))
    return x2.reshape(batch, seq, D_MODEL)
```
